```python
import jax, jax.numpy as jnp
from jax import lax
import numpy as np

D_MODEL = 1024
BATCH = 8
SEQ = 2048
DEPTH = 2

N_MIXERS = 2
NORM_EPS = 1e-6

GDN_K_HEADS = 8
GDN_V_HEADS = 16
GDN_HEAD_K = 128
GDN_HEAD_V = 128
GDN_CONV = 4
GDN_CHUNK = 64
GDN_QK = GDN_K_HEADS * GDN_HEAD_K
GDN_V = GDN_V_HEADS * GDN_HEAD_V
GDN_CONV_DIM = 2 * GDN_QK + GDN_V
GDN_PROJ = GDN_CONV_DIM + GDN_V + 2 * GDN_V_HEADS

SC_WIDTH = 2 * D_MODEL
SC_CONV = 3
SC_PROJ = 4 * SC_WIDTH

N_GDN_LAYERS = (DEPTH + 1) // 2
N_SC_LAYERS = DEPTH // 2

kernel_name = "hybrid_gdn_shortconv_sandwich"


def rms_norm(x, w):
    xf = x.astype(jnp.float32)
    y = xf * lax.rsqrt(jnp.mean(xf * xf, axis=-1, keepdims=True) + NORM_EPS)
    return y * w.astype(jnp.float32)


def l2_normalize(x):
    xf = x.astype(jnp.float32)
    return xf * lax.rsqrt(jnp.sum(xf * xf, axis=-1, keepdims=True) + NORM_EPS)


def causal_depthwise_conv(x, w):
    width = w.shape[0]
    t = x.shape[1]
    xp = jnp.pad(x, ((0, 0), (width - 1, 0), (0, 0)))
    return sum(w[j] * xp[:, j:j + t] for j in range(width))


def chunked_gated_delta_rule(q, k, v, g, beta):
    b, h, t, dk = q.shape
    dv = v.shape[-1]
    c = GDN_CHUNK
    n = t // c
    q = q * (dk ** -0.5)
    kb = k * beta[..., None]
    vb = v * beta[..., None]
    q, k, kb = [a.reshape(b, h, n, c, dk) for a in (q, k, kb)]
    vb = vb.reshape(b, h, n, c, dv)
    g = jnp.cumsum(g.reshape(b, h, n, c), axis=-1)
    incl = jnp.tril(jnp.ones((c, c), dtype=bool))
    strict = jnp.tril(jnp.ones((c, c), dtype=bool), k=-1)
    diff = g[..., :, None] - g[..., None, :]
    decay = jnp.where(incl, jnp.exp(jnp.where(incl, diff, 0.0)), 0.0)
    a_mat = jnp.where(strict, jnp.einsum('bhnid,bhnjd->bhnij', kb, k) * decay, 0.0)
    ia = a_mat + jnp.eye(c, dtype=a_mat.dtype)
    u = lax.linalg.triangular_solve(ia, vb, left_side=True, lower=True, unit_diagonal=True)
    w = lax.linalg.triangular_solve(ia, kb * jnp.exp(g)[..., None], left_side=True, lower=True,
                                    unit_diagonal=True)
    qk = jnp.where(incl, jnp.einsum('bhnid,bhnjd->bhnij', q, k) * decay, 0.0)
    q_dec = q * jnp.exp(g)[..., None]
    g_last = g[..., -1]
    k_dec = k * jnp.exp(g_last[..., None] - g)[..., None]
    state_decay = jnp.exp(g_last)

    def step(s, xs):
        q_c, k_c, u_c, w_c, qk_c, sd_c = xs
        v_new = u_c - jnp.einsum('bhck,bhkv->bhcv', w_c, s)
        o_c = jnp.einsum('bhck,bhkv->bhcv', q_c, s) + jnp.einsum('bhij,bhjv->bhiv', qk_c, v_new)
        s = s * sd_c[..., None, None] + jnp.einsum('bhck,bhcv->bhkv', k_c, v_new)
        return s, o_c

    xs = tuple(jnp.moveaxis(a, 2, 0) for a in (q_dec, k_dec, u, w, qk, state_decay))
    s0 = jnp.zeros((b, h, dk, dv), dtype=jnp.float32)
    _, o = lax.scan(step, s0, xs)
    return jnp.moveaxis(o, 0, 2).reshape(b, h, t, dv)


def gated_deltanet_mixer(h, w_in, conv_w, a_log, dt_bias, norm_w, w_out):
    bsz, t, _ = h.shape
    proj = h @ w_in
    qkv, z, b_logit, a_logit = jnp.split(
        proj, [GDN_CONV_DIM, GDN_CONV_DIM + GDN_V, GDN_CONV_DIM + GDN_V + GDN_V_HEADS], axis=-1)
    qkv = jax.nn.silu(causal_depthwise_conv(qkv, conv_w))
    q, k, v = jnp.split(qkv, [GDN_QK, 2 * GDN_QK], axis=-1)
    rep = GDN_V_HEADS // GDN_K_HEADS
    q = jnp.repeat(l2_normalize(q.reshape(bsz, t, GDN_K_HEADS, GDN_HEAD_K)), rep, axis=2)
    k = jnp.repeat(l2_normalize(k.reshape(bsz, t, GDN_K_HEADS, GDN_HEAD_K)), rep, axis=2)
    v = v.reshape(bsz, t, GDN_V_HEADS, GDN_HEAD_V).astype(jnp.float32)
    beta = jax.nn.sigmoid(b_logit.astype(jnp.float32))
    g = -jnp.exp(a_log.astype(jnp.float32)) * jax.nn.softplus(
        a_logit.astype(jnp.float32) + dt_bias.astype(jnp.float32))
    to_bht = lambda a: jnp.swapaxes(a, 1, 2)
    o = chunked_gated_delta_rule(to_bht(q), to_bht(k), to_bht(v), to_bht(g), to_bht(beta))
    o = jnp.swapaxes(o, 1, 2)
    gate = jax.nn.silu(z.reshape(bsz, t, GDN_V_HEADS, GDN_HEAD_V).astype(jnp.float32))
    o = rms_norm(o, norm_w) * gate
    return o.reshape(bsz, t, GDN_V).astype(h.dtype) @ w_out


def short_conv_mixer(h, w_in, conv_w, w_out):
    u, b_gate, c_gate, z = jnp.split(h @ w_in, 4, axis=-1)
    y = b_gate * causal_depthwise_conv(c_gate * u, conv_w)
    return (y * jax.nn.silu(z)) @ w_out


def setup_inputs(seed: int = 0) -> dict:
    key = jax.random.key(seed)
    ks = jax.random.split(key, 12)
    f32 = jnp.float32
    x = jax.random.normal(ks[0], (BATCH, SEQ, D_MODEL), f32)
    pre_norm_w = 1.0 + 0.05 * jax.random.normal(ks[1], (DEPTH, D_MODEL), f32)
    post_norm_w = 1.0 + 0.05 * jax.random.normal(ks[2], (DEPTH, D_MODEL), f32)
    gdn_w_in = jax.random.normal(ks[3], (N_GDN_LAYERS, D_MODEL, GDN_PROJ), f32) * D_MODEL ** -0.5
    gdn_conv_w = jax.random.normal(ks[4], (N_GDN_LAYERS, GDN_CONV, GDN_CONV_DIM), f32) * GDN_CONV ** -0.5
    gdn_A_log = jnp.log(jax.random.uniform(ks[5], (N_GDN_LAYERS, GDN_V_HEADS), f32, 1.0, 16.0))
    dt = jnp.exp(jax.random.uniform(ks[6], (N_GDN_LAYERS, GDN_V_HEADS), f32,
                                    float(np.log(1e-3)), float(np.log(1e-1))))
    gdn_dt_bias = dt + jnp.log(-jnp.expm1(-dt))
    gdn_norm_w = 1.0 + 0.05 * jax.random.normal(ks[7], (N_GDN_LAYERS, GDN_HEAD_V), f32)
    gdn_w_out = jax.random.normal(ks[8], (N_GDN_LAYERS, GDN_V, D_MODEL), f32) * GDN_V ** -0.5
    sc_w_in = jax.random.normal(ks[9], (N_SC_LAYERS, D_MODEL, SC_PROJ), f32) * D_MODEL ** -0.5
    sc_conv_w = jax.random.normal(ks[10], (N_SC_LAYERS, SC_CONV, SC_WIDTH), f32) * SC_CONV ** -0.5
    sc_w_out = jax.random.normal(ks[11], (N_SC_LAYERS, SC_WIDTH, D_MODEL), f32) * SC_WIDTH ** -0.5
    return {"x": x, "pre_norm_w": pre_norm_w, "post_norm_w": post_norm_w,
            "gdn_w_in": gdn_w_in, "gdn_conv_w": gdn_conv_w, "gdn_A_log": gdn_A_log,
            "gdn_dt_bias": gdn_dt_bias, "gdn_norm_w": gdn_norm_w, "gdn_w_out": gdn_w_out,
            "sc_w_in": sc_w_in, "sc_conv_w": sc_conv_w, "sc_w_out": sc_w_out}


def reference(x, pre_norm_w, post_norm_w, gdn_w_in, gdn_conv_w, gdn_A_log, gdn_dt_bias, gdn_norm_w,
              gdn_w_out, sc_w_in, sc_conv_w, sc_w_out):
    for i in range(DEPTH):
        h = rms_norm(x, pre_norm_w[i]).astype(x.dtype)
        j = i // N_MIXERS
        if i % N_MIXERS == 0:
            y = gated_deltanet_mixer(h, gdn_w_in[j], gdn_conv_w[j], gdn_A_log[j], gdn_dt_bias[j],
                                     gdn_norm_w[j], gdn_w_out[j])
        else:
            y = short_conv_mixer(h, sc_w_in[j], sc_conv_w[j], sc_w_out[j])
        x = x + rms_norm(y, post_norm_w[i]).astype(x.dtype)
    return x
```

```python
import numpy as np
import jax
import jax.numpy as jnp
from jax import lax
from jax.experimental import pallas as pl
from jax.experimental.pallas import tpu as pltpu

D = 1024
EPS = 1e-6
HK, HV, DK, DV = 8, 16, 128, 128
QK = HK * DK
V = HV * DV
CONVD = 2 * QK + V
C = 64
NP = HV // 2
TT = 256
NCT = TT // C
W = 2048
TTS = 512
CG = 512
LANES = 128
SUBLANES = 8
VMEM_LIMIT = 56 * 1024 * 1024
HI = lax.Precision.HIGHEST
f32 = jnp.float32
bf16 = jnp.bfloat16


def _dot(a, b, **kw):
    return jnp.dot(a, b, preferred_element_type=f32, **kw)


def _rms(x, w):
    return x * lax.rsqrt(jnp.mean(x * x, axis=-1, keepdims=True) + EPS) * w


def gdn_kernel(x_ref, prew_ref, postw_ref, wqkvz_ref, wb_ref, wa_ref, wbT_ref, waT_ref, cw_ref,
               alog_row_ref, dt_row_ref, alog_col_ref, dt_col_ref,
               dup_ref, u2_ref, ones2_ref, ltri_ref, cmask_ref, lvl_ref, eye_ref,
               normw_ref, wout_ref,
               o_ref,
               xp_s, z_s, q_s, k_s, kT2_s, v_s, S_s, lh1_s, lh2_s,
               bcol_s, gcol_s, brow_s, gcrow_s, glrow_s, og_s):
    t = pl.program_id(1)

    @pl.when(t == 0)
    def _():
        S_s[...] = jnp.zeros(S_s.shape, f32)
        xp_s[pl.ds(0, SUBLANES), :] = jnp.zeros((SUBLANES, CONVD), f32)

    x = x_ref[0]
    hb = _rms(x, prew_ref[...]).astype(bf16)

    GW = 1024
    for j in range(CONVD // GW):
        xp_s[pl.ds(SUBLANES, TT), j * GW:(j + 1) * GW] = _dot(hb, wqkvz_ref[:, j * GW:(j + 1) * GW])
    for j in range(V // GW):
        z_s[:, j * GW:(j + 1) * GW] = _dot(hb, wqkvz_ref[:, CONVD + j * GW:CONVD + (j + 1) * GW])

    nt = (((1,), (1,)), ((), ()))
    b_col = _dot(hb, wb_ref[...])
    a_col = _dot(hb, wa_ref[...])
    bT = lax.dot_general(wbT_ref[...], hb, nt, preferred_element_type=f32)
    aT = lax.dot_general(waT_ref[...], hb, nt, preferred_element_type=f32)
    g_col = -jnp.exp(alog_row_ref[...]) * jax.nn.softplus(a_col + dt_row_ref[...])
    gT = -jnp.exp(alog_col_ref[...]) * jax.nn.softplus(aT + dt_col_ref[...])
    betaT = jax.nn.sigmoid(bT)
    bcol_s[...] = jax.nn.sigmoid(b_col)
    gcol_s[...] = _dot(ltri_ref[...], g_col, precision=HI)
    brow_s[...] = _dot(betaT, dup_ref[...], precision=HI)
    gcrow_s[...] = _dot(gT, u2_ref[...], precision=HI)
    glrow_s[...] = _dot(gT, ones2_ref[...], precision=HI)

    def conv_group(g):
        cs = pl.ds(pl.multiple_of(g * LANES, LANES), LANES)
        acc = cw_ref[3:4, cs] * xp_s[pl.ds(SUBLANES, TT), cs]
        acc += cw_ref[2:3, cs] * xp_s[pl.ds(SUBLANES - 1, TT), cs]
        acc += cw_ref[1:2, cs] * xp_s[pl.ds(SUBLANES - 2, TT), cs]
        acc += cw_ref[0:1, cs] * xp_s[pl.ds(SUBLANES - 3, TT), cs]
        return acc * jax.nn.sigmoid(acc)

    def l2n(a):
        return a * lax.rsqrt(jnp.sum(a * a, axis=-1, keepdims=True) + EPS)

    def q_body(g, _):
        cs = pl.ds(pl.multiple_of(g * LANES, LANES), LANES)
        q_s[:, cs] = (l2n(conv_group(g)) * (DK ** -0.5)).astype(bf16)
        return 0
    lax.fori_loop(0, HK, q_body, 0)

    def k_body(g, _):
        cs = pl.ds(pl.multiple_of(g * LANES, LANES), LANES)
        kn = l2n(conv_group(g + HK))
        k_s[:, cs] = kn.astype(bf16)
        for c in range(NCT):
            kc = kn[c * C:(c + 1) * C]
            kT2_s[g, :, c * LANES:(c + 1) * LANES] = jnp.concatenate([kc, kc], axis=0).T.astype(bf16)
        return 0
    lax.fori_loop(0, HK, k_body, 0)

    def v_body(g, _):
        cs = pl.ds(pl.multiple_of(g * LANES, LANES), LANES)
        v_s[:, cs] = conv_group(g + 2 * HK).astype(bf16)
        return 0
    lax.fori_loop(0, HV, v_body, 0)

    xp_s[pl.ds(0, SUBLANES), :] = xp_s[pl.ds(TT, SUBLANES), :]

    lowm = cmask_ref[0]
    strictm = cmask_ref[1]
    diagm = cmask_ref[2]
    left = cmask_ref[3]
    right = cmask_ref[4]

    def pair_rows(ref, p, c):
        cs = pl.ds(pl.multiple_of(c * LANES, LANES), LANES)
        return ref[2 * p:2 * p + 1, cs], ref[2 * p + 1:2 * p + 2, cs]

    def phase1(c, _):
        rs = pl.ds(pl.multiple_of(c * C, C), C)
        cl = pl.ds(pl.multiple_of(c * LANES, LANES), LANES)
        for p in range(NP):
            hs = slice(p * DK, (p + 1) * DK)
            qk = jnp.concatenate([q_s[rs, hs], k_s[rs, hs]], axis=0)
            G = _dot(qk, kT2_s[p, :, cl])
            QKp, KKp = G[:C], G[C:]
            gr0, gr1 = pair_rows(gcrow_s, p, c)
            br0, br1 = pair_rows(brow_s, p, c)
            l1 = left[0:1]
            grow = jnp.where(l1 > 0, gr0, gr1)
            brow = jnp.where(l1 > 0, br0, br1)
            gcol = jnp.where(left > 0, gcol_s[rs, 2 * p:2 * p + 1], gcol_s[rs, 2 * p + 1:2 * p + 2])
            bcol = jnp.where(left > 0, bcol_s[rs, 2 * p:2 * p + 1], bcol_s[rs, 2 * p + 1:2 * p + 2])
            Dm = jnp.exp((gcol - grow) * lowm) * lowm
            Ap = bcol * KKp * Dm * strictm
            Abd = jnp.concatenate([Ap * left, Ap * right], axis=0)
            Ab = Abd.astype(bf16)
            X = eye_ref[...] - Abd * lvl_ref[0].astype(f32)
            for lv in range(1, 6):
                Xb = X.astype(bf16)
                Tm = _dot(Ab * lvl_ref[lv], Xb)
                X = X - _dot(Xb, Tm.astype(bf16))
            Y = X[:C] + X[C:]
            Yb = Y * brow
            eg = jnp.exp(grow)
            lh1_s[c, p] = jnp.concatenate([Yb, -(Yb * eg)], axis=1).astype(bf16)
            lh2_s[c, p] = jnp.concatenate([QKp * Dm, diagm * eg], axis=1).astype(bf16)
        return 0
    lax.fori_loop(0, NCT, phase1, 0)

    zero = jnp.zeros((C, LANES), bf16)

    def bd(a):
        return jnp.concatenate([jnp.concatenate([a[:, :LANES], zero], axis=1),
                                jnp.concatenate([zero, a[:, LANES:]], axis=1)], axis=0)

    def phase2(c, _):
        rs = pl.ds(pl.multiple_of(c * C, C), C)
        cl = pl.ds(pl.multiple_of(c * LANES, LANES), LANES)
        for p in range(NP):
            hs = slice(p * DK, (p + 1) * DK)
            S = S_s[p]
            qk = jnp.concatenate([q_s[rs, hs], k_s[rs, hs]], axis=0)
            qkS = _dot(qk, S.astype(bf16))
            qS, kS = qkS[:C].astype(bf16), qkS[C:].astype(bf16)
            vp = v_s[rs, p * 2 * DV:(p + 1) * 2 * DV]
            vnew = _dot(lh1_s[c, p], jnp.concatenate([bd(vp), bd(kS)], axis=0)).astype(bf16)
            vbd = bd(vnew)
            o = _dot(lh2_s[c, p], jnp.concatenate([vbd, bd(qS)], axis=0))
            gr0, gr1 = pair_rows(gcrow_s, p, c)
            gl0, gl1 = pair_rows(glrow_s, p, c)
            l1 = left[0:1]
            r = jnp.exp(jnp.where(l1 > 0, gl0 - gr0, gl1 - gr1))
            kTr = (kT2_s[p, :, cl].astype(f32) * r).astype(bf16)
            dec = jnp.concatenate([jnp.exp(gl0), jnp.exp(gl1)], axis=1)
            S_s[p] = S * dec + _dot(kTr, vbd)
            for hh in range(2):
                oh = o[:, hh * DV:(hh + 1) * DV]
                zc = pl.ds(p * 2 * DV + hh * DV, DV)
                zz = z_s[rs, zc]
                og_s[rs, zc] = (_rms(oh, normw_ref[...]) * (zz * jax.nn.sigmoid(zz))).astype(bf16)
        return 0
    lax.fori_loop(0, NCT, phase2, 0)

    y = _dot(og_s[...], wout_ref[...])
    o_ref[0] = x + _rms(y, postw_ref[...])


def _gdn_constants():
    tt = np.arange(TT)
    m2 = np.arange(2 * TT)
    src = (m2 // LANES) * C + (m2 % C)
    same = (tt[:, None] // C) == (src[None, :] // C)
    dup = (tt[:, None] == src[None, :]).astype(np.float32)
    u2 = (same & (tt[:, None] <= src[None, :])).astype(np.float32)
    ones2 = same.astype(np.float32)
    ltri = (((tt[:, None] // C) == (tt[None, :] // C)) & (tt[None, :] <= tt[:, None])).astype(np.float32)
    r = np.arange(C)[:, None]
    l = np.arange(LANES)[None, :]
    cm = np.stack([(r >= l % C), (r > l % C), (r == l % C), np.broadcast_to(l < C, (C, LANES)),
                   np.broadcast_to(l >= C, (C, LANES))]).astype(np.float32)
    i = np.arange(2 * C)[:, None]
    j = np.arange(2 * C)[None, :]
    lv = np.stack([((i >> (s + 1)) == (j >> (s + 1))) & ((i >> s) != (j >> s)) & (i > j) for s in range(6)])
    return dup, u2, ones2, ltri, cm, lv.astype(np.float32), np.eye(2 * C, dtype=np.float32)


def _const_spec(a):
    nd = a.ndim
    return pl.BlockSpec(a.shape, lambda b, t, _n=nd: (0,) * _n, pipeline_mode=pl.Buffered(1))


def gdn_layer(x, prew, postw, w_in, conv_w, a_log, dt_bias, norm_w, w_out):
    B, T, _ = x.shape
    assert T % TT == 0
    dup, u2, ones2, ltri, cm, lv, eye = _gdn_constants()
    wqkvz = w_in[:, :CONVD + V].astype(bf16)
    wb = w_in[:, CONVD + V:CONVD + V + HV].astype(bf16)
    wa = w_in[:, CONVD + V + HV:].astype(bf16)
    args = (x, prew.reshape(1, D), postw.reshape(1, D), wqkvz, wb, wa, wb.T, wa.T, conv_w,
            a_log.reshape(1, HV), dt_bias.reshape(1, HV), a_log.reshape(HV, 1), dt_bias.reshape(HV, 1),
            jnp.asarray(dup), jnp.asarray(u2), jnp.asarray(ones2), jnp.asarray(ltri), jnp.asarray(cm),
            jnp.asarray(lv, dtype=bf16), jnp.asarray(eye), norm_w.reshape(1, DV), w_out.astype(bf16))
    in_specs = [pl.BlockSpec((1, TT, D), lambda b, t: (b, t, 0))] + [_const_spec(a) for a in args[1:]]
    return pl.pallas_call(
        gdn_kernel,
        grid=(B, T // TT),
        in_specs=in_specs,
        out_specs=pl.BlockSpec((1, TT, D), lambda b, t: (b, t, 0)),
        out_shape=jax.ShapeDtypeStruct(x.shape, f32),
        scratch_shapes=[
            pltpu.VMEM((TT + SUBLANES, CONVD), f32),
            pltpu.VMEM((TT, V), f32),
            pltpu.VMEM((TT, QK), bf16),
            pltpu.VMEM((TT, QK), bf16),
            pltpu.VMEM((HK, DK, NCT * LANES), bf16),
            pltpu.VMEM((TT, V), bf16),
            pltpu.VMEM((NP, DK, 2 * DV), f32),
            pltpu.VMEM((NCT, NP, C, 2 * LANES), bf16),
            pltpu.VMEM((NCT, NP, C, 2 * LANES), bf16),
            pltpu.VMEM((TT, HV), f32),
            pltpu.VMEM((TT, HV), f32),
            pltpu.VMEM((HV, 2 * TT), f32),
            pltpu.VMEM((HV, 2 * TT), f32),
            pltpu.VMEM((HV, 2 * TT), f32),
            pltpu.VMEM((TT, V), bf16)],
        compiler_params=pltpu.CompilerParams(dimension_semantics=("arbitrary", "arbitrary"),
                                             vmem_limit_bytes=VMEM_LIMIT),
        name="gdn_layer",
    )(*args)


def sc_kernel(x_ref, prew_ref, postw_ref, win_ref, cw_ref, wout_ref, o_ref, cu_s, y_s):
    t = pl.program_id(1)

    @pl.when(t == 0)
    def _():
        cu_s[pl.ds(0, SUBLANES), :] = jnp.zeros((SUBLANES, W), f32)

    x = x_ref[0]
    hb = _rms(x, prew_ref[...]).astype(bf16)
    for j in range(W // CG):
        cs = slice(j * CG, (j + 1) * CG)
        u = _dot(hb, win_ref[:, j * CG:(j + 1) * CG])
        cg = _dot(hb, win_ref[:, 2 * W + j * CG:2 * W + (j + 1) * CG])
        cu_s[pl.ds(SUBLANES, TTS), cs] = cg * u
        bg = _dot(hb, win_ref[:, W + j * CG:W + (j + 1) * CG])
        z = _dot(hb, win_ref[:, 3 * W + j * CG:3 * W + (j + 1) * CG])
        conv = (cw_ref[2:3, cs] * cu_s[pl.ds(SUBLANES, TTS), cs]
                + cw_ref[1:2, cs] * cu_s[pl.ds(SUBLANES - 1, TTS), cs]
                + cw_ref[0:1, cs] * cu_s[pl.ds(SUBLANES - 2, TTS), cs])
        y_s[:, cs] = (bg * conv * (z * jax.nn.sigmoid(z))).astype(bf16)
    cu_s[pl.ds(0, SUBLANES), :] = cu_s[pl.ds(TTS, SUBLANES), :]
    yo = _dot(y_s[...], wout_ref[...])
    o_ref[0] = x + _rms(yo, postw_ref[...])


def sc_layer(x, prew, postw, w_in, conv_w, w_out):
    B, T, _ = x.shape
    assert T % TTS == 0
    args = (x, prew.reshape(1, D), postw.reshape(1, D), w_in.astype(bf16), conv_w, w_out.astype(bf16))
    in_specs = [pl.BlockSpec((1, TTS, D), lambda b, t: (b, t, 0))] + [_const_spec(a) for a in args[1:]]
    return pl.pallas_call(
        sc_kernel,
        grid=(B, T // TTS),
        in_specs=in_specs,
        out_specs=pl.BlockSpec((1, TTS, D), lambda b, t: (b, t, 0)),
        out_shape=jax.ShapeDtypeStruct(x.shape, f32),
        scratch_shapes=[pltpu.VMEM((TTS + SUBLANES, W), f32), pltpu.VMEM((TTS, W), bf16)],
        compiler_params=pltpu.CompilerParams(dimension_semantics=("arbitrary", "arbitrary"),
                                             vmem_limit_bytes=VMEM_LIMIT),
        name="sc_layer",
    )(*args)


def kernel(x, pre_norm_w, post_norm_w, gdn_w_in, gdn_conv_w, gdn_A_log, gdn_dt_bias, gdn_norm_w, gdn_w_out,
           sc_w_in, sc_conv_w, sc_w_out):
    x = gdn_layer(x, pre_norm_w[0], post_norm_w[0], gdn_w_in[0], gdn_conv_w[0], gdn_A_log[0],
                  gdn_dt_bias[0], gdn_norm_w[0], gdn_w_out[0])
    x = sc_layer(x, pre_norm_w[1], post_norm_w[1], sc_w_in[0], sc_conv_w[0], sc_w_out[0])
    return x
```

```python
import numpy as np
import jax
import jax.numpy as jnp
from jax import lax
from jax.experimental import pallas as pl
from jax.experimental.pallas import tpu as pltpu

D = 1024
EPS = 1e-6
HK, HV, DK, DV = 8, 16, 128, 128
QK = HK * DK
V = HV * DV
CONVD = 2 * QK + V
C = 64
NP = HV // 2
TT = 256
NCT = TT // C
P1_CHUNKS = 2
W = 2048
TTS = 512
CG = 512
LANES = 128
SUBLANES = 8
VMEM_LIMIT = 56 * 1024 * 1024
HI = lax.Precision.HIGHEST
f32 = jnp.float32
bf16 = jnp.bfloat16


def _dot(a, b, **kw):
    return jnp.dot(a, b, preferred_element_type=f32, **kw)


def _rms(x, w):
    return x * lax.rsqrt(jnp.mean(x * x, axis=-1, keepdims=True) + EPS) * w


def gdn_kernel(x_ref, prew_ref, postw_ref, wqkvz_ref, wb_ref, wa_ref, wbT_ref, waT_ref, cw_ref,
               alog_row_ref, dt_row_ref, alog_col_ref, dt_col_ref,
               dup_ref, u2_ref, ones2_ref, ltri_ref, cmask_ref, lvl_ref, eye_ref,
               normw_ref, wout_ref,
               o_ref,
               xp_s, z_s, q_s, k_s, kT2_s, v_s, S_s, lh1_s, lh2_s,
               bcol_s, gcol_s, brow_s, gcrow_s, glrow_s, og_s):
    t = pl.program_id(1)

    @pl.when(t == 0)
    def _():
        S_s[...] = jnp.zeros(S_s.shape, f32)
        xp_s[pl.ds(0, SUBLANES), :] = jnp.zeros((SUBLANES, CONVD), f32)

    x = x_ref[0]
    hb = _rms(x, prew_ref[...]).astype(bf16)

    GW = 1024
    for j in range(CONVD // GW):
        xp_s[pl.ds(SUBLANES, TT), j * GW:(j + 1) * GW] = _dot(hb, wqkvz_ref[:, j * GW:(j + 1) * GW])
    for j in range(V // GW):
        z_s[:, j * GW:(j + 1) * GW] = _dot(hb, wqkvz_ref[:, CONVD + j * GW:CONVD + (j + 1) * GW])

    nt = (((1,), (1,)), ((), ()))
    b_col = _dot(hb, wb_ref[...])
    a_col = _dot(hb, wa_ref[...])
    bT = lax.dot_general(wbT_ref[...], hb, nt, preferred_element_type=f32)
    aT = lax.dot_general(waT_ref[...], hb, nt, preferred_element_type=f32)
    g_col = -jnp.exp(alog_row_ref[...]) * jax.nn.softplus(a_col + dt_row_ref[...])
    gT = -jnp.exp(alog_col_ref[...]) * jax.nn.softplus(aT + dt_col_ref[...])
    betaT = jax.nn.sigmoid(bT)
    bcol_s[...] = jax.nn.sigmoid(b_col)
    gcol_s[...] = _dot(ltri_ref[...], g_col, precision=HI)
    brow_s[...] = _dot(betaT, dup_ref[...], precision=HI)
    gcrow_s[...] = _dot(gT, u2_ref[...], precision=HI)
    glrow_s[...] = _dot(gT, ones2_ref[...], precision=HI)

    def conv_group(g):
        cs = pl.ds(pl.multiple_of(g * LANES, LANES), LANES)
        acc = cw_ref[3:4, cs] * xp_s[pl.ds(SUBLANES, TT), cs]
        acc += cw_ref[2:3, cs] * xp_s[pl.ds(SUBLANES - 1, TT), cs]
        acc += cw_ref[1:2, cs] * xp_s[pl.ds(SUBLANES - 2, TT), cs]
        acc += cw_ref[0:1, cs] * xp_s[pl.ds(SUBLANES - 3, TT), cs]
        return acc * jax.nn.sigmoid(acc)

    def l2n(a):
        return a * lax.rsqrt(jnp.sum(a * a, axis=-1, keepdims=True) + EPS)

    def q_body(g, _):
        cs = pl.ds(pl.multiple_of(g * LANES, LANES), LANES)
        q_s[:, cs] = (l2n(conv_group(g)) * (DK ** -0.5)).astype(bf16)
        return 0
    lax.fori_loop(0, HK, q_body, 0)

    def k_body(g, _):
        cs = pl.ds(pl.multiple_of(g * LANES, LANES), LANES)
        kn = l2n(conv_group(g + HK))
        k_s[:, cs] = kn.astype(bf16)
        for c in range(NCT):
            kc = kn[c * C:(c + 1) * C]
            kT2_s[g, :, c * LANES:(c + 1) * LANES] = jnp.concatenate([kc, kc], axis=0).T.astype(bf16)
        return 0
    lax.fori_loop(0, HK, k_body, 0)

    def v_body(g, _):
        cs = pl.ds(pl.multiple_of(g * LANES, LANES), LANES)
        v_s[:, cs] = conv_group(g + 2 * HK).astype(bf16)
        return 0
    lax.fori_loop(0, HV, v_body, 0)

    xp_s[pl.ds(0, SUBLANES), :] = xp_s[pl.ds(TT, SUBLANES), :]

    lowm = cmask_ref[0]
    strictm = cmask_ref[1]
    diagm = cmask_ref[2]
    left = cmask_ref[3]
    right = cmask_ref[4]

    def pair_rows(ref, p, c):
        cs = pl.ds(pl.multiple_of(c * LANES, LANES), LANES)
        return ref[2 * p:2 * p + 1, cs], ref[2 * p + 1:2 * p + 2, cs]

    def phase1(chains):
        Abs, Xs, tails = [], [], []
        for c, p in chains:
            rs = slice(c * C, (c + 1) * C)
            cl = slice(c * LANES, (c + 1) * LANES)
            hs = slice(p * DK, (p + 1) * DK)
            qk = jnp.concatenate([q_s[rs, hs], k_s[rs, hs]], axis=0)
            G = _dot(qk, kT2_s[p, :, cl])
            QKp, KKp = G[:C], G[C:]
            l1 = left[0:1]
            grow = jnp.where(l1 > 0, gcrow_s[2 * p:2 * p + 1, cl], gcrow_s[2 * p + 1:2 * p + 2, cl])
            brow = jnp.where(l1 > 0, brow_s[2 * p:2 * p + 1, cl], brow_s[2 * p + 1:2 * p + 2, cl])
            gcol = jnp.where(left > 0, gcol_s[rs, 2 * p:2 * p + 1], gcol_s[rs, 2 * p + 1:2 * p + 2])
            bcol = jnp.where(left > 0, bcol_s[rs, 2 * p:2 * p + 1], bcol_s[rs, 2 * p + 1:2 * p + 2])
            Dm = jnp.exp((gcol - grow) * lowm) * lowm
            Ap = bcol * KKp * Dm * strictm
            Abd = jnp.concatenate([Ap * left, Ap * right], axis=0)
            Abs.append(Abd.astype(bf16))
            Xs.append(eye_ref[...] - Abd * lvl_ref[0].astype(f32))
            eg = jnp.exp(grow)
            lh2_s[c, p] = jnp.concatenate([QKp * Dm, diagm * eg], axis=1).astype(bf16)
            tails.append((brow, eg))
        for lv in range(1, 6):
            Xbs = [X.astype(bf16) for X in Xs]
            Tms = [_dot(Ab * lvl_ref[lv], Xb).astype(bf16) for Ab, Xb in zip(Abs, Xbs)]
            Xs = [X - _dot(Xb, Tm) for X, Xb, Tm in zip(Xs, Xbs, Tms)]
        for (c, p), X, (brow, eg) in zip(chains, Xs, tails):
            Yb = (X[:C] + X[C:]) * brow
            lh1_s[c, p] = jnp.concatenate([Yb, -(Yb * eg)], axis=1).astype(bf16)

    for c0 in range(0, NCT, P1_CHUNKS):
        phase1([(c, p) for c in range(c0, c0 + P1_CHUNKS) for p in range(NP)])

    zero = jnp.zeros((C, LANES), bf16)

    def bd(a):
        return jnp.concatenate([jnp.concatenate([a[:, :LANES], zero], axis=1),
                                jnp.concatenate([zero, a[:, LANES:]], axis=1)], axis=0)

    def phase2(c, _):
        rs = pl.ds(pl.multiple_of(c * C, C), C)
        cl = pl.ds(pl.multiple_of(c * LANES, LANES), LANES)
        pairs = range(NP)
        Ss = [S_s[p] for p in pairs]
        qkSs = [_dot(jnp.concatenate([q_s[rs, p * DK:(p + 1) * DK], k_s[rs, p * DK:(p + 1) * DK]], axis=0),
                     Ss[p].astype(bf16)) for p in pairs]
        vnews = [_dot(lh1_s[c, p], jnp.concatenate([bd(v_s[rs, p * 2 * DV:(p + 1) * 2 * DV]),
                                                    bd(qkSs[p][C:].astype(bf16))], axis=0)).astype(bf16)
                 for p in pairs]
        vbds = [bd(vn) for vn in vnews]
        os_ = [_dot(lh2_s[c, p], jnp.concatenate([vbds[p], bd(qkSs[p][:C].astype(bf16))], axis=0)) for p in pairs]
        l1 = left[0:1]
        for p in pairs:
            gr0, gr1 = pair_rows(gcrow_s, p, c)
            gl0, gl1 = pair_rows(glrow_s, p, c)
            r = jnp.exp(jnp.where(l1 > 0, gl0 - gr0, gl1 - gr1))
            kTr = (kT2_s[p, :, cl].astype(f32) * r).astype(bf16)
            dec = jnp.concatenate([jnp.exp(gl0), jnp.exp(gl1)], axis=1)
            S_s[p] = Ss[p] * dec + _dot(kTr, vbds[p])
        for p in pairs:
            for hh in range(2):
                oh = os_[p][:, hh * DV:(hh + 1) * DV]
                zc = pl.ds(p * 2 * DV + hh * DV, DV)
                zz = z_s[rs, zc]
                og_s[rs, zc] = (_rms(oh, normw_ref[...]) * (zz * jax.nn.sigmoid(zz))).astype(bf16)
        return 0
    lax.fori_loop(0, NCT, phase2, 0)

    y = _dot(og_s[...], wout_ref[...])
    o_ref[0] = x + _rms(y, postw_ref[...])


def _gdn_constants():
    tt = np.arange(TT)
    m2 = np.arange(2 * TT)
    src = (m2 // LANES) * C + (m2 % C)
    same = (tt[:, None] // C) == (src[None, :] // C)
    dup = (tt[:, None] == src[None, :]).astype(np.float32)
    u2 = (same & (tt[:, None] <= src[None, :])).astype(np.float32)
    ones2 = same.astype(np.float32)
    ltri = (((tt[:, None] // C) == (tt[None, :] // C)) & (tt[None, :] <= tt[:, None])).astype(np.float32)
    r = np.arange(C)[:, None]
    l = np.arange(LANES)[None, :]
    cm = np.stack([(r >= l % C), (r > l % C), (r == l % C), np.broadcast_to(l < C, (C, LANES)),
                   np.broadcast_to(l >= C, (C, LANES))]).astype(np.float32)
    i = np.arange(2 * C)[:, None]
    j = np.arange(2 * C)[None, :]
    lv = np.stack([((i >> (s + 1)) == (j >> (s + 1))) & ((i >> s) != (j >> s)) & (i > j) for s in range(6)])
    return dup, u2, ones2, ltri, cm, lv.astype(np.float32), np.eye(2 * C, dtype=np.float32)


def _const_spec(a):
    nd = a.ndim
    return pl.BlockSpec(a.shape, lambda b, t, _n=nd: (0,) * _n, pipeline_mode=pl.Buffered(1))


def gdn_layer(x, prew, postw, w_in, conv_w, a_log, dt_bias, norm_w, w_out):
    B, T, _ = x.shape
    assert T % TT == 0
    dup, u2, ones2, ltri, cm, lv, eye = _gdn_constants()
    wqkvz = w_in[:, :CONVD + V].astype(bf16)
    wb = w_in[:, CONVD + V:CONVD + V + HV].astype(bf16)
    wa = w_in[:, CONVD + V + HV:].astype(bf16)
    args = (x, prew.reshape(1, D), postw.reshape(1, D), wqkvz, wb, wa, wb.T, wa.T, conv_w,
            a_log.reshape(1, HV), dt_bias.reshape(1, HV), a_log.reshape(HV, 1), dt_bias.reshape(HV, 1),
            jnp.asarray(dup), jnp.asarray(u2), jnp.asarray(ones2), jnp.asarray(ltri), jnp.asarray(cm),
            jnp.asarray(lv, dtype=bf16), jnp.asarray(eye), norm_w.reshape(1, DV), w_out.astype(bf16))
    in_specs = [pl.BlockSpec((1, TT, D), lambda b, t: (b, t, 0))] + [_const_spec(a) for a in args[1:]]
    return pl.pallas_call(
        gdn_kernel,
        grid=(B, T // TT),
        in_specs=in_specs,
        out_specs=pl.BlockSpec((1, TT, D), lambda b, t: (b, t, 0)),
        out_shape=jax.ShapeDtypeStruct(x.shape, f32),
        scratch_shapes=[
            pltpu.VMEM((TT + SUBLANES, CONVD), f32),
            pltpu.VMEM((TT, V), f32),
            pltpu.VMEM((TT, QK), bf16),
            pltpu.VMEM((TT, QK), bf16),
            pltpu.VMEM((HK, DK, NCT * LANES), bf16),
            pltpu.VMEM((TT, V), bf16),
            pltpu.VMEM((NP, DK, 2 * DV), f32),
            pltpu.VMEM((NCT, NP, C, 2 * LANES), bf16),
            pltpu.VMEM((NCT, NP, C, 2 * LANES), bf16),
            pltpu.VMEM((TT, HV), f32),
            pltpu.VMEM((TT, HV), f32),
            pltpu.VMEM((HV, 2 * TT), f32),
            pltpu.VMEM((HV, 2 * TT), f32),
            pltpu.VMEM((HV, 2 * TT), f32),
            pltpu.VMEM((TT, V), bf16)],
        compiler_params=pltpu.CompilerParams(dimension_semantics=("arbitrary", "arbitrary"),
                                             vmem_limit_bytes=VMEM_LIMIT),
        name="gdn_layer",
    )(*args)


def sc_kernel(x_ref, prew_ref, postw_ref, win_ref, cw_ref, wout_ref, o_ref, cu_s, y_s):
    t = pl.program_id(1)

    @pl.when(t == 0)
    def _():
        cu_s[pl.ds(0, SUBLANES), :] = jnp.zeros((SUBLANES, W), f32)

    x = x_ref[0]
    hb = _rms(x, prew_ref[...]).astype(bf16)
    for j in range(W // CG):
        cs = slice(j * CG, (j + 1) * CG)
        u = _dot(hb, win_ref[:, j * CG:(j + 1) * CG])
        cg = _dot(hb, win_ref[:, 2 * W + j * CG:2 * W + (j + 1) * CG])
        cu_s[pl.ds(SUBLANES, TTS), cs] = cg * u
        bg = _dot(hb, win_ref[:, W + j * CG:W + (j + 1) * CG])
        z = _dot(hb, win_ref[:, 3 * W + j * CG:3 * W + (j + 1) * CG])
        conv = (cw_ref[2:3, cs] * cu_s[pl.ds(SUBLANES, TTS), cs]
                + cw_ref[1:2, cs] * cu_s[pl.ds(SUBLANES - 1, TTS), cs]
                + cw_ref[0:1, cs] * cu_s[pl.ds(SUBLANES - 2, TTS), cs])
        y_s[:, cs] = (bg * conv * (z * jax.nn.sigmoid(z))).astype(bf16)
    cu_s[pl.ds(0, SUBLANES), :] = cu_s[pl.ds(TTS, SUBLANES), :]
    yo = _dot(y_s[...], wout_ref[...])
    o_ref[0] = x + _rms(yo, postw_ref[...])


def sc_layer(x, prew, postw, w_in, conv_w, w_out):
    B, T, _ = x.shape
    assert T % TTS == 0
    args = (x, prew.reshape(1, D), postw.reshape(1, D), w_in.astype(bf16), conv_w, w_out.astype(bf16))
    in_specs = [pl.BlockSpec((1, TTS, D), lambda b, t: (b, t, 0))] + [_const_spec(a) for a in args[1:]]
    return pl.pallas_call(
        sc_kernel,
        grid=(B, T // TTS),
        in_specs=in_specs,
        out_specs=pl.BlockSpec((1, TTS, D), lambda b, t: (b, t, 0)),
        out_shape=jax.ShapeDtypeStruct(x.shape, f32),
        scratch_shapes=[pltpu.VMEM((TTS + SUBLANES, W), f32), pltpu.VMEM((TTS, W), bf16)],
        compiler_params=pltpu.CompilerParams(dimension_semantics=("arbitrary", "arbitrary"),
                                             vmem_limit_bytes=VMEM_LIMIT),
        name="sc_layer",
    )(*args)


def kernel(x, pre_norm_w, post_norm_w, gdn_w_in, gdn_conv_w, gdn_A_log, gdn_dt_bias, gdn_norm_w, gdn_w_out,
           sc_w_in, sc_conv_w, sc_w_out):
    x = gdn_layer(x, pre_norm_w[0], post_norm_w[0], gdn_w_in[0], gdn_conv_w[0], gdn_A_log[0],
                  gdn_dt_bias[0], gdn_norm_w[0], gdn_w_out[0])
    x = sc_layer(x, pre_norm_w[1], post_norm_w[1], sc_w_in[0], sc_conv_w[0], sc_w_out[0])
    return x
```

```python
import numpy as np
import jax
import jax.numpy as jnp
from jax import lax
from jax.experimental import pallas as pl
from jax.experimental.pallas import tpu as pltpu

D = 1024
EPS = 1e-6
HK, HV, DK, DV = 8, 16, 128, 128
QK = HK * DK
V = HV * DV
CONVD = 2 * QK + V
C = 64
NP = HV // 2
TT = 256
NCT = TT // C
P1_CHUNKS = 2
PG = 512
W = 2048
TTS = 512
CG = 512
LANES = 128
SUBLANES = 8
VMEM_LIMIT = 56 * 1024 * 1024
HI = lax.Precision.HIGHEST
f32 = jnp.float32
bf16 = jnp.bfloat16


def _dot(a, b, **kw):
    return jnp.dot(a, b, preferred_element_type=f32, **kw)


def _rms(x, w):
    return x * lax.rsqrt(jnp.mean(x * x, axis=-1, keepdims=True) + EPS) * w


def gdn_kernel(x_ref, prew_ref, postw_ref, wqkvz_ref, wb_ref, wa_ref, wbT_ref, waT_ref, cw_ref,
               alog_row_ref, dt_row_ref, alog_col_ref, dt_col_ref,
               dup_ref, u2_ref, ones2_ref, ltri_ref, cmask_ref, lvl_ref, eye_ref,
               normw_ref, wout_ref,
               o_ref,
               xp_s, z_s, q_s, k_s, kT2_s, v_s, S_s, lh1_s, lh2_s,
               bcol_s, gcol_s, brow_s, gcrow_s, glrow_s, og_s):
    t = pl.program_id(1)

    @pl.when(t == 0)
    def _():
        S_s[...] = jnp.zeros(S_s.shape, f32)
        xp_s[pl.ds(0, SUBLANES), :] = jnp.zeros((SUBLANES, CONVD), f32)

    x = x_ref[0]
    hb = _rms(x, prew_ref[...]).astype(bf16)

    nt = (((1,), (1,)), ((), ()))
    b_col = _dot(hb, wb_ref[...])
    a_col = _dot(hb, wa_ref[...])
    bT = lax.dot_general(wbT_ref[...], hb, nt, preferred_element_type=f32)
    aT = lax.dot_general(waT_ref[...], hb, nt, preferred_element_type=f32)
    g_col = -jnp.exp(alog_row_ref[...]) * jax.nn.softplus(a_col + dt_row_ref[...])
    gT = -jnp.exp(alog_col_ref[...]) * jax.nn.softplus(aT + dt_col_ref[...])
    betaT = jax.nn.sigmoid(bT)
    bcol_s[...] = jax.nn.sigmoid(b_col)
    gcol_s[...] = _dot(ltri_ref[...], g_col, precision=HI)
    brow_s[...] = _dot(betaT, dup_ref[...], precision=HI)
    gcrow_s[...] = _dot(gT, u2_ref[...], precision=HI)
    glrow_s[...] = _dot(gT, ones2_ref[...], precision=HI)

    def l2n(a):
        return a * lax.rsqrt(jnp.sum(a * a, axis=-1, keepdims=True) + EPS)

    for j in range(CONVD // PG):
        gs = slice(j * PG, (j + 1) * PG)
        xp_s[pl.ds(SUBLANES, TT), gs] = _dot(hb, wqkvz_ref[:, gs])
        for i in range(PG // LANES):
            col = j * PG + i * LANES
            cs = slice(col, col + LANES)
            acc = cw_ref[3:4, cs] * xp_s[pl.ds(SUBLANES, TT), cs]
            acc += cw_ref[2:3, cs] * xp_s[pl.ds(SUBLANES - 1, TT), cs]
            acc += cw_ref[1:2, cs] * xp_s[pl.ds(SUBLANES - 2, TT), cs]
            acc += cw_ref[0:1, cs] * xp_s[pl.ds(SUBLANES - 3, TT), cs]
            a = acc * jax.nn.sigmoid(acc)
            if col < QK:
                q_s[:, cs] = (l2n(a) * (DK ** -0.5)).astype(bf16)
            elif col < 2 * QK:
                g = (col - QK) // DK
                kn = l2n(a)
                k_s[:, col - QK:col - QK + LANES] = kn.astype(bf16)
                for c in range(NCT):
                    kc = kn[c * C:(c + 1) * C]
                    kT2_s[g, :, c * LANES:(c + 1) * LANES] = jnp.concatenate([kc, kc], axis=0).T.astype(bf16)
            else:
                v_s[:, col - 2 * QK:col - 2 * QK + LANES] = a.astype(bf16)
    for j in range(V // PG):
        gs = slice(j * PG, (j + 1) * PG)
        zz = _dot(hb, wqkvz_ref[:, CONVD + j * PG:CONVD + (j + 1) * PG])
        z_s[:, gs] = zz * jax.nn.sigmoid(zz)

    xp_s[pl.ds(0, SUBLANES), :] = xp_s[pl.ds(TT, SUBLANES), :]

    lowm = cmask_ref[0]
    strictm = cmask_ref[1]
    diagm = cmask_ref[2]
    left = cmask_ref[3]
    right = cmask_ref[4]

    def pair_rows(ref, p, c):
        cs = pl.ds(pl.multiple_of(c * LANES, LANES), LANES)
        return ref[2 * p:2 * p + 1, cs], ref[2 * p + 1:2 * p + 2, cs]

    def phase1(chains):
        Abs, Xs, tails = [], [], []
        for c, p in chains:
            rs = slice(c * C, (c + 1) * C)
            cl = slice(c * LANES, (c + 1) * LANES)
            hs = slice(p * DK, (p + 1) * DK)
            qk = jnp.concatenate([q_s[rs, hs], k_s[rs, hs]], axis=0)
            G = _dot(qk, kT2_s[p, :, cl])
            QKp, KKp = G[:C], G[C:]
            l1 = left[0:1]
            grow = jnp.where(l1 > 0, gcrow_s[2 * p:2 * p + 1, cl], gcrow_s[2 * p + 1:2 * p + 2, cl])
            brow = jnp.where(l1 > 0, brow_s[2 * p:2 * p + 1, cl], brow_s[2 * p + 1:2 * p + 2, cl])
            gcol = jnp.where(left > 0, gcol_s[rs, 2 * p:2 * p + 1], gcol_s[rs, 2 * p + 1:2 * p + 2])
            bcol = jnp.where(left > 0, bcol_s[rs, 2 * p:2 * p + 1], bcol_s[rs, 2 * p + 1:2 * p + 2])
            Dm = jnp.exp((gcol - grow) * lowm) * lowm
            Ap = bcol * KKp * Dm * strictm
            Abd = jnp.concatenate([Ap * left, Ap * right], axis=0)
            Abs.append(Abd.astype(bf16))
            Xs.append(eye_ref[...] - Abd * lvl_ref[0].astype(f32))
            eg = jnp.exp(grow)
            lh2_s[c, p] = jnp.concatenate([QKp * Dm, diagm * eg], axis=1).astype(bf16)
            tails.append((brow, eg))
        for lv in range(1, 6):
            Xbs = [X.astype(bf16) for X in Xs]
            Tms = [_dot(Ab * lvl_ref[lv], Xb).astype(bf16) for Ab, Xb in zip(Abs, Xbs)]
            Xs = [X - _dot(Xb, Tm) for X, Xb, Tm in zip(Xs, Xbs, Tms)]
        for (c, p), X, (brow, eg) in zip(chains, Xs, tails):
            Yb = (X[:C] + X[C:]) * brow
            lh1_s[c, p] = jnp.concatenate([Yb, -(Yb * eg)], axis=1).astype(bf16)

    for c0 in range(0, NCT, P1_CHUNKS):
        phase1([(c, p) for c in range(c0, c0 + P1_CHUNKS) for p in range(NP)])

    zero = jnp.zeros((C, LANES), bf16)

    def bd(a):
        return jnp.concatenate([jnp.concatenate([a[:, :LANES], zero], axis=1),
                                jnp.concatenate([zero, a[:, LANES:]], axis=1)], axis=0)

    def phase2(c, _):
        rs = pl.ds(pl.multiple_of(c * C, C), C)
        cl = pl.ds(pl.multiple_of(c * LANES, LANES), LANES)
        pairs = range(NP)
        Ss = [S_s[p] for p in pairs]
        qkSs = [_dot(jnp.concatenate([q_s[rs, p * DK:(p + 1) * DK], k_s[rs, p * DK:(p + 1) * DK]], axis=0),
                     Ss[p].astype(bf16)) for p in pairs]
        vnews = [_dot(lh1_s[c, p], jnp.concatenate([bd(v_s[rs, p * 2 * DV:(p + 1) * 2 * DV]),
                                                    bd(qkSs[p][C:].astype(bf16))], axis=0)).astype(bf16)
                 for p in pairs]
        vbds = [bd(vn) for vn in vnews]
        os_ = [_dot(lh2_s[c, p], jnp.concatenate([vbds[p], bd(qkSs[p][:C].astype(bf16))], axis=0)) for p in pairs]
        l1 = left[0:1]
        for p in pairs:
            gr0, gr1 = pair_rows(gcrow_s, p, c)
            gl0, gl1 = pair_rows(glrow_s, p, c)
            r = jnp.exp(jnp.where(l1 > 0, gl0 - gr0, gl1 - gr1))
            kTr = (kT2_s[p, :, cl].astype(f32) * r).astype(bf16)
            dec = jnp.concatenate([jnp.exp(gl0), jnp.exp(gl1)], axis=1)
            S_s[p] = Ss[p] * dec + _dot(kTr, vbds[p])
        for p in pairs:
            for hh in range(2):
                oh = os_[p][:, hh * DV:(hh + 1) * DV]
                zc = pl.ds(p * 2 * DV + hh * DV, DV)
                og_s[rs, zc] = (_rms(oh, normw_ref[...]) * z_s[rs, zc]).astype(bf16)
        return 0
    lax.fori_loop(0, NCT, phase2, 0)

    y = _dot(og_s[...], wout_ref[...])
    o_ref[0] = x + _rms(y, postw_ref[...])


def _gdn_constants():
    tt = np.arange(TT)
    m2 = np.arange(2 * TT)
    src = (m2 // LANES) * C + (m2 % C)
    same = (tt[:, None] // C) == (src[None, :] // C)
    dup = (tt[:, None] == src[None, :]).astype(np.float32)
    u2 = (same & (tt[:, None] <= src[None, :])).astype(np.float32)
    ones2 = same.astype(np.float32)
    ltri = (((tt[:, None] // C) == (tt[None, :] // C)) & (tt[None, :] <= tt[:, None])).astype(np.float32)
    r = np.arange(C)[:, None]
    l = np.arange(LANES)[None, :]
    cm = np.stack([(r >= l % C), (r > l % C), (r == l % C), np.broadcast_to(l < C, (C, LANES)),
                   np.broadcast_to(l >= C, (C, LANES))]).astype(np.float32)
    i = np.arange(2 * C)[:, None]
    j = np.arange(2 * C)[None, :]
    lv = np.stack([((i >> (s + 1)) == (j >> (s + 1))) & ((i >> s) != (j >> s)) & (i > j) for s in range(6)])
    return dup, u2, ones2, ltri, cm, lv.astype(np.float32), np.eye(2 * C, dtype=np.float32)


def _const_spec(a):
    nd = a.ndim
    return pl.BlockSpec(a.shape, lambda b, t, _n=nd: (0,) * _n, pipeline_mode=pl.Buffered(1))


def gdn_layer(x, prew, postw, w_in, conv_w, a_log, dt_bias, norm_w, w_out):
    B, T, _ = x.shape
    assert T % TT == 0
    dup, u2, ones2, ltri, cm, lv, eye = _gdn_constants()
    wqkvz = w_in[:, :CONVD + V].astype(bf16)
    wb = w_in[:, CONVD + V:CONVD + V + HV].astype(bf16)
    wa = w_in[:, CONVD + V + HV:].astype(bf16)
    args = (x, prew.reshape(1, D), postw.reshape(1, D), wqkvz, wb, wa, wb.T, wa.T, conv_w,
            a_log.reshape(1, HV), dt_bias.reshape(1, HV), a_log.reshape(HV, 1), dt_bias.reshape(HV, 1),
            jnp.asarray(dup), jnp.asarray(u2), jnp.asarray(ones2), jnp.asarray(ltri), jnp.asarray(cm),
            jnp.asarray(lv, dtype=bf16), jnp.asarray(eye), norm_w.reshape(1, DV), w_out.astype(bf16))
    in_specs = [pl.BlockSpec((1, TT, D), lambda b, t: (b, t, 0))] + [_const_spec(a) for a in args[1:]]
    return pl.pallas_call(
        gdn_kernel,
        grid=(B, T // TT),
        in_specs=in_specs,
        out_specs=pl.BlockSpec((1, TT, D), lambda b, t: (b, t, 0)),
        out_shape=jax.ShapeDtypeStruct(x.shape, f32),
        scratch_shapes=[
            pltpu.VMEM((TT + SUBLANES, CONVD), f32),
            pltpu.VMEM((TT, V), f32),
            pltpu.VMEM((TT, QK), bf16),
            pltpu.VMEM((TT, QK), bf16),
            pltpu.VMEM((HK, DK, NCT * LANES), bf16),
            pltpu.VMEM((TT, V), bf16),
            pltpu.VMEM((NP, DK, 2 * DV), f32),
            pltpu.VMEM((NCT, NP, C, 2 * LANES), bf16),
            pltpu.VMEM((NCT, NP, C, 2 * LANES), bf16),
            pltpu.VMEM((TT, HV), f32),
            pltpu.VMEM((TT, HV), f32),
            pltpu.VMEM((HV, 2 * TT), f32),
            pltpu.VMEM((HV, 2 * TT), f32),
            pltpu.VMEM((HV, 2 * TT), f32),
            pltpu.VMEM((TT, V), bf16)],
        compiler_params=pltpu.CompilerParams(dimension_semantics=("arbitrary", "arbitrary"),
                                             vmem_limit_bytes=VMEM_LIMIT),
        name="gdn_layer",
    )(*args)


def sc_kernel(x_ref, prew_ref, postw_ref, win_ref, cw_ref, wout_ref, o_ref, cu_s, y_s):
    t = pl.program_id(1)

    @pl.when(t == 0)
    def _():
        cu_s[pl.ds(0, SUBLANES), :] = jnp.zeros((SUBLANES, W), f32)

    x = x_ref[0]
    hb = _rms(x, prew_ref[...]).astype(bf16)
    for j in range(W // CG):
        cs = slice(j * CG, (j + 1) * CG)
        u = _dot(hb, win_ref[:, j * CG:(j + 1) * CG])
        cg = _dot(hb, win_ref[:, 2 * W + j * CG:2 * W + (j + 1) * CG])
        cu_s[pl.ds(SUBLANES, TTS), cs] = cg * u
        bg = _dot(hb, win_ref[:, W + j * CG:W + (j + 1) * CG])
        z = _dot(hb, win_ref[:, 3 * W + j * CG:3 * W + (j + 1) * CG])
        conv = (cw_ref[2:3, cs] * cu_s[pl.ds(SUBLANES, TTS), cs]
                + cw_ref[1:2, cs] * cu_s[pl.ds(SUBLANES - 1, TTS), cs]
                + cw_ref[0:1, cs] * cu_s[pl.ds(SUBLANES - 2, TTS), cs])
        y_s[:, cs] = (bg * conv * (z * jax.nn.sigmoid(z))).astype(bf16)
    cu_s[pl.ds(0, SUBLANES), :] = cu_s[pl.ds(TTS, SUBLANES), :]
    yo = _dot(y_s[...], wout_ref[...])
    o_ref[0] = x + _rms(yo, postw_ref[...])


def sc_layer(x, prew, postw, w_in, conv_w, w_out):
    B, T, _ = x.shape
    assert T % TTS == 0
    args = (x, prew.reshape(1, D), postw.reshape(1, D), w_in.astype(bf16), conv_w, w_out.astype(bf16))
    in_specs = [pl.BlockSpec((1, TTS, D), lambda b, t: (b, t, 0))] + [_const_spec(a) for a in args[1:]]
    return pl.pallas_call(
        sc_kernel,
        grid=(B, T // TTS),
        in_specs=in_specs,
        out_specs=pl.BlockSpec((1, TTS, D), lambda b, t: (b, t, 0)),
        out_shape=jax.ShapeDtypeStruct(x.shape, f32),
        scratch_shapes=[pltpu.VMEM((TTS + SUBLANES, W), f32), pltpu.VMEM((TTS, W), bf16)],
        compiler_params=pltpu.CompilerParams(dimension_semantics=("arbitrary", "arbitrary"),
                                             vmem_limit_bytes=VMEM_LIMIT),
        name="sc_layer",
    )(*args)


def kernel(x, pre_norm_w, post_norm_w, gdn_w_in, gdn_conv_w, gdn_A_log, gdn_dt_bias, gdn_norm_w, gdn_w_out,
           sc_w_in, sc_conv_w, sc_w_out):
    x = gdn_layer(x, pre_norm_w[0], post_norm_w[0], gdn_w_in[0], gdn_conv_w[0], gdn_A_log[0],
                  gdn_dt_bias[0], gdn_norm_w[0], gdn_w_out[0])
    x = sc_layer(x, pre_norm_w[1], post_norm_w[1], sc_w_in[0], sc_conv_w[0], sc_w_out[0])
    return x
```

```python
import numpy as np
import jax
import jax.numpy as jnp
from jax import lax
from jax.experimental import pallas as pl
from jax.experimental.pallas import tpu as pltpu

D = 1024
EPS = 1e-6
HK, HV, DK, DV = 8, 16, 128, 128
QK = HK * DK
V = HV * DV
CONVD = 2 * QK + V
C = 64
NP = HV // 2
TT = 256
NCT = TT // C
P1_CHUNKS = 2
PG = 512
W = 2048
TTS = 512
CG = 512
LANES = 128
SUBLANES = 8
VMEM_LIMIT = 56 * 1024 * 1024
f32 = jnp.float32
bf16 = jnp.bfloat16


def _dot(a, b, **kw):
    return jnp.dot(a, b, preferred_element_type=f32, **kw)


def _rms(x, w):
    return x * lax.rsqrt(jnp.mean(x * x, axis=-1, keepdims=True) + EPS) * w


def _split3(a):
    hi = a.astype(bf16)
    r1 = a - hi.astype(f32)
    mid = r1.astype(bf16)
    lo = (r1 - mid.astype(f32)).astype(bf16)
    return hi, mid, lo


def _split3_rows(a):
    return jnp.concatenate(_split3(a), axis=0)


def _fold3_rows(r):
    n = r.shape[0] // 3
    return r[:n] + r[n:2 * n] + r[2 * n:]


def _dot3_rows(a, m):
    return _fold3_rows(_dot(_split3_rows(a), m))


def _interleave(*gens):
    gens = list(gens)
    while gens:
        for g in list(gens):
            try:
                next(g)
            except StopIteration:
                gens.remove(g)


def gdn_kernel(x_ref, prew_ref, postw_ref, wqkvz_ref, wb_ref, wa_ref, wbT_ref, waT_ref, cw_ref,
               alog_row_ref, dt_row_ref, alog_col_ref, dt_col_ref,
               dup_ref, u2_ref, ones2_ref, ltri_ref, cmask_ref, lvl_ref,
               normw_ref, wout_ref,
               o_ref,
               xp_s, z_s, q_s, k_s, kT2_s, v_s, S_s, lh1_s, lh2_s,
               bcol_s, gcol_s, brow_s, gcrow_s, glrow_s, og_s):
    t = pl.program_id(1)

    @pl.when(t == 0)
    def _():
        S_s[...] = jnp.zeros(S_s.shape, f32)
        xp_s[pl.ds(0, SUBLANES), :] = jnp.zeros((SUBLANES, CONVD), f32)

    x = x_ref[0]
    hb = _rms(x, prew_ref[...]).astype(bf16)

    nt = (((1,), (1,)), ((), ()))
    b_col = _dot(hb, wb_ref[...])
    a_col = _dot(hb, wa_ref[...])
    bT = lax.dot_general(wbT_ref[...], hb, nt, preferred_element_type=f32)
    aT = lax.dot_general(waT_ref[...], hb, nt, preferred_element_type=f32)
    g_col = -jnp.exp(alog_row_ref[...]) * jax.nn.softplus(a_col + dt_row_ref[...])
    gT = -jnp.exp(alog_col_ref[...]) * jax.nn.softplus(aT + dt_col_ref[...])
    betaT = jax.nn.sigmoid(bT)
    bcol_s[...] = jax.nn.sigmoid(b_col)
    gc_parts = _split3(g_col)
    gcol_s[...] = sum(_dot(ltri_ref[...], part) for part in gc_parts)
    brow_s[...] = _dot3_rows(betaT, dup_ref[...])
    gT3 = _split3_rows(gT)
    gcrow_s[...] = _fold3_rows(_dot(gT3, u2_ref[...]))
    glrow_s[...] = _fold3_rows(_dot(gT3, ones2_ref[...]))

    def l2n(a):
        return a * lax.rsqrt(jnp.sum(a * a, axis=-1, keepdims=True) + EPS)

    def project(groups):
        for j in groups:
            gs = slice(j * PG, (j + 1) * PG)
            if j * PG >= CONVD:
                zz = _dot(hb, wqkvz_ref[:, gs])
                z_s[:, j * PG - CONVD:(j + 1) * PG - CONVD] = zz * jax.nn.sigmoid(zz)
                yield
                continue
            xp_s[pl.ds(SUBLANES, TT), gs] = _dot(hb, wqkvz_ref[:, gs])
            yield
            for i in range(PG // LANES):
                col = j * PG + i * LANES
                cs = slice(col, col + LANES)
                acc = cw_ref[3:4, cs] * xp_s[pl.ds(SUBLANES, TT), cs]
                acc += cw_ref[2:3, cs] * xp_s[pl.ds(SUBLANES - 1, TT), cs]
                acc += cw_ref[1:2, cs] * xp_s[pl.ds(SUBLANES - 2, TT), cs]
                acc += cw_ref[0:1, cs] * xp_s[pl.ds(SUBLANES - 3, TT), cs]
                a = acc * jax.nn.sigmoid(acc)
                if col < QK:
                    q_s[:, cs] = (l2n(a) * (DK ** -0.5)).astype(bf16)
                elif col < 2 * QK:
                    g = (col - QK) // DK
                    kn = l2n(a)
                    k_s[:, col - QK:col - QK + LANES] = kn.astype(bf16)
                    for c in range(NCT):
                        kc = kn[c * C:(c + 1) * C]
                        kT2_s[g, :, c * LANES:(c + 1) * LANES] = jnp.concatenate([kc, kc], axis=0).T.astype(bf16)
                else:
                    v_s[:, col - 2 * QK:col - 2 * QK + LANES] = a.astype(bf16)
                yield

    n_qk = 2 * QK // PG
    for _ in project(range(n_qk)):
        pass

    lowm = cmask_ref[0]
    strictm = cmask_ref[1]
    diagm = cmask_ref[2]
    left = cmask_ref[3]
    right = cmask_ref[4]
    leftb, rightb = left.astype(bf16), right.astype(bf16)

    def bd1(a):
        return jnp.concatenate([a * leftb, a * rightb], axis=0)

    def pair_rows(ref, p, c):
        cs = slice(c * LANES, (c + 1) * LANES)
        return ref[2 * p:2 * p + 1, cs], ref[2 * p + 1:2 * p + 2, cs]

    def phase1(chains):
        Abs, Xs, tails = [], [], []
        for c, p in chains:
            rs = slice(c * C, (c + 1) * C)
            cl = slice(c * LANES, (c + 1) * LANES)
            hs = slice(p * DK, (p + 1) * DK)
            qk = jnp.concatenate([q_s[rs, hs], k_s[rs, hs]], axis=0)
            G = _dot(qk, kT2_s[p, :, cl])
            QKp, KKp = G[:C], G[C:]
            l1 = left[0:1]
            grow = jnp.where(l1 > 0, gcrow_s[2 * p:2 * p + 1, cl], gcrow_s[2 * p + 1:2 * p + 2, cl])
            brow = jnp.where(l1 > 0, brow_s[2 * p:2 * p + 1, cl], brow_s[2 * p + 1:2 * p + 2, cl])
            gcol = jnp.where(left > 0, gcol_s[rs, 2 * p:2 * p + 1], gcol_s[rs, 2 * p + 1:2 * p + 2])
            bcol = jnp.where(left > 0, bcol_s[rs, 2 * p:2 * p + 1], bcol_s[rs, 2 * p + 1:2 * p + 2])
            Dm = jnp.exp((gcol - grow) * lowm) * lowm
            Ap = bcol * KKp * Dm * strictm
            Abs.append(Ap.astype(bf16))
            Xs.append(diagm - Ap * lvl_ref[0].astype(f32))
            eg = jnp.exp(grow)
            lh2_s[c, p] = jnp.concatenate([QKp * Dm, diagm * eg], axis=1).astype(bf16)
            tails.append((brow, eg))
            if len(Xs) % 4 == 0:
                yield
        for lv in range(1, 6):
            Xbs = [X.astype(bf16) for X in Xs]
            Tms = [_dot(Ab * lvl_ref[lv], bd1(Xb)).astype(bf16) for Ab, Xb in zip(Abs, Xbs)]
            yield
            Xs = [X - _dot(Xb, bd1(Tm)) for X, Xb, Tm in zip(Xs, Xbs, Tms)]
            yield
        for n, ((c, p), X, (brow, eg)) in enumerate(zip(chains, Xs, tails)):
            Yb = X * brow
            lh1_s[c, p] = jnp.concatenate([Yb, -(Yb * eg)], axis=1).astype(bf16)
            if n % 8 == 7:
                yield

    def phase1_all():
        for c0 in range(0, NCT, P1_CHUNKS):
            yield from phase1([(c, p) for c in range(c0, c0 + P1_CHUNKS) for p in range(NP)])

    _interleave(phase1_all(), project(range(n_qk, (CONVD + V) // PG)))

    xp_s[pl.ds(0, SUBLANES), :] = xp_s[pl.ds(TT, SUBLANES), :]

    zero = jnp.zeros((C, LANES), bf16)

    def bd(a):
        return jnp.concatenate([jnp.concatenate([a[:, :LANES], zero], axis=1),
                                jnp.concatenate([zero, a[:, LANES:]], axis=1)], axis=0)

    def phase2(c):
        rs = slice(c * C, (c + 1) * C)
        cl = slice(c * LANES, (c + 1) * LANES)
        pairs = range(NP)
        Ss = [S_s[p] for p in pairs]
        qkSs = [_dot(jnp.concatenate([q_s[rs, p * DK:(p + 1) * DK], k_s[rs, p * DK:(p + 1) * DK]], axis=0),
                     Ss[p].astype(bf16)) for p in pairs]
        vnews = [_dot(lh1_s[c, p], jnp.concatenate([bd(v_s[rs, p * 2 * DV:(p + 1) * 2 * DV]),
                                                    bd(qkSs[p][C:].astype(bf16))], axis=0)).astype(bf16)
                 for p in pairs]
        vbds = [bd(vn) for vn in vnews]
        os_ = [_dot(lh2_s[c, p], jnp.concatenate([vbds[p], bd(qkSs[p][:C].astype(bf16))], axis=0)) for p in pairs]
        l1 = left[0:1]
        for p in pairs:
            gr0, gr1 = pair_rows(gcrow_s, p, c)
            gl0, gl1 = pair_rows(glrow_s, p, c)
            r = jnp.exp(jnp.where(l1 > 0, gl0 - gr0, gl1 - gr1))
            kTr = (kT2_s[p, :, cl].astype(f32) * r).astype(bf16)
            dec = jnp.concatenate([jnp.exp(gl0), jnp.exp(gl1)], axis=1)
            S_s[p] = Ss[p] * dec + _dot(kTr, vbds[p])
        for p in pairs:
            for hh in range(2):
                oh = os_[p][:, hh * DV:(hh + 1) * DV]
                zc = slice(p * 2 * DV + hh * DV, p * 2 * DV + (hh + 1) * DV)
                og_s[rs, zc] = (_rms(oh, normw_ref[...]) * z_s[rs, zc]).astype(bf16)

    for c in range(NCT):
        phase2(c)

    y = _dot(og_s[...], wout_ref[...])
    o_ref[0] = x + _rms(y, postw_ref[...])


def _gdn_constants():
    tt = np.arange(TT)
    m2 = np.arange(2 * TT)
    src = (m2 // LANES) * C + (m2 % C)
    same = (tt[:, None] // C) == (src[None, :] // C)
    dup = (tt[:, None] == src[None, :]).astype(np.float32)
    u2 = (same & (tt[:, None] <= src[None, :])).astype(np.float32)
    ones2 = same.astype(np.float32)
    ltri = (((tt[:, None] // C) == (tt[None, :] // C)) & (tt[None, :] <= tt[:, None])).astype(np.float32)
    r = np.arange(C)[:, None]
    l = np.arange(LANES)[None, :]
    cm = np.stack([(r >= l % C), (r > l % C), (r == l % C), np.broadcast_to(l < C, (C, LANES)),
                   np.broadcast_to(l >= C, (C, LANES))]).astype(np.float32)
    j = l % C
    lv = np.stack([((r >> (s + 1)) == (j >> (s + 1))) & ((r >> s) != (j >> s)) & (r > j) for s in range(6)])
    return dup, u2, ones2, ltri, cm, lv.astype(np.float32)


def _const_spec(a):
    nd = a.ndim
    return pl.BlockSpec(a.shape, lambda b, t, _n=nd: (0,) * _n, pipeline_mode=pl.Buffered(1))


def gdn_layer(x, prew, postw, w_in, conv_w, a_log, dt_bias, norm_w, w_out):
    B, T, _ = x.shape
    assert T % TT == 0
    dup, u2, ones2, ltri, cm, lv = _gdn_constants()
    wqkvz = w_in[:, :CONVD + V].astype(bf16)
    wb = w_in[:, CONVD + V:CONVD + V + HV].astype(bf16)
    wa = w_in[:, CONVD + V + HV:].astype(bf16)
    args = (x, prew.reshape(1, D), postw.reshape(1, D), wqkvz, wb, wa, wb.T, wa.T, conv_w,
            a_log.reshape(1, HV), dt_bias.reshape(1, HV), a_log.reshape(HV, 1), dt_bias.reshape(HV, 1),
            jnp.asarray(dup, dtype=bf16), jnp.asarray(u2, dtype=bf16), jnp.asarray(ones2, dtype=bf16),
            jnp.asarray(ltri, dtype=bf16), jnp.asarray(cm),
            jnp.asarray(lv, dtype=bf16), norm_w.reshape(1, DV), w_out.astype(bf16))
    in_specs = [pl.BlockSpec((1, TT, D), lambda b, t: (b, t, 0))] + [_const_spec(a) for a in args[1:]]
    return pl.pallas_call(
        gdn_kernel,
        grid=(B, T // TT),
        in_specs=in_specs,
        out_specs=pl.BlockSpec((1, TT, D), lambda b, t: (b, t, 0)),
        out_shape=jax.ShapeDtypeStruct(x.shape, f32),
        scratch_shapes=[
            pltpu.VMEM((TT + SUBLANES, CONVD), f32),
            pltpu.VMEM((TT, V), f32),
            pltpu.VMEM((TT, QK), bf16),
            pltpu.VMEM((TT, QK), bf16),
            pltpu.VMEM((HK, DK, NCT * LANES), bf16),
            pltpu.VMEM((TT, V), bf16),
            pltpu.VMEM((NP, DK, 2 * DV), f32),
            pltpu.VMEM((NCT, NP, C, 2 * LANES), bf16),
            pltpu.VMEM((NCT, NP, C, 2 * LANES), bf16),
            pltpu.VMEM((TT, HV), f32),
            pltpu.VMEM((TT, HV), f32),
            pltpu.VMEM((HV, 2 * TT), f32),
            pltpu.VMEM((HV, 2 * TT), f32),
            pltpu.VMEM((HV, 2 * TT), f32),
            pltpu.VMEM((TT, V), bf16)],
        compiler_params=pltpu.CompilerParams(dimension_semantics=("arbitrary", "arbitrary"),
                                             vmem_limit_bytes=VMEM_LIMIT),
        name="gdn_layer",
    )(*args)


def sc_kernel(x_ref, prew_ref, postw_ref, win_ref, cw_ref, wout_ref, o_ref, cu_s, y_s):
    t = pl.program_id(1)

    @pl.when(t == 0)
    def _():
        cu_s[pl.ds(0, SUBLANES), :] = jnp.zeros((SUBLANES, W), f32)

    x = x_ref[0]
    hb = _rms(x, prew_ref[...]).astype(bf16)
    for j in range(W // CG):
        cs = slice(j * CG, (j + 1) * CG)
        u = _dot(hb, win_ref[:, j * CG:(j + 1) * CG])
        cg = _dot(hb, win_ref[:, 2 * W + j * CG:2 * W + (j + 1) * CG])
        cu_s[pl.ds(SUBLANES, TTS), cs] = cg * u
        bg = _dot(hb, win_ref[:, W + j * CG:W + (j + 1) * CG])
        z = _dot(hb, win_ref[:, 3 * W + j * CG:3 * W + (j + 1) * CG])
        conv = (cw_ref[2:3, cs] * cu_s[pl.ds(SUBLANES, TTS), cs]
                + cw_ref[1:2, cs] * cu_s[pl.ds(SUBLANES - 1, TTS), cs]
                + cw_ref[0:1, cs] * cu_s[pl.ds(SUBLANES - 2, TTS), cs])
        y_s[:, cs] = (bg * conv * (z * jax.nn.sigmoid(z))).astype(bf16)
    cu_s[pl.ds(0, SUBLANES), :] = cu_s[pl.ds(TTS, SUBLANES), :]
    yo = _dot(y_s[...], wout_ref[...])
    o_ref[0] = x + _rms(yo, postw_ref[...])


def sc_layer(x, prew, postw, w_in, conv_w, w_out):
    B, T, _ = x.shape
    assert T % TTS == 0
    args = (x, prew.reshape(1, D), postw.reshape(1, D), w_in.astype(bf16), conv_w, w_out.astype(bf16))
    in_specs = [pl.BlockSpec((1, TTS, D), lambda b, t: (b, t, 0))] + [_const_spec(a) for a in args[1:]]
    return pl.pallas_call(
        sc_kernel,
        grid=(B, T // TTS),
        in_specs=in_specs,
        out_specs=pl.BlockSpec((1, TTS, D), lambda b, t: (b, t, 0)),
        out_shape=jax.ShapeDtypeStruct(x.shape, f32),
        scratch_shapes=[pltpu.VMEM((TTS + SUBLANES, W), f32), pltpu.VMEM((TTS, W), bf16)],
        compiler_params=pltpu.CompilerParams(dimension_semantics=("arbitrary", "arbitrary"),
                                             vmem_limit_bytes=VMEM_LIMIT),
        name="sc_layer",
    )(*args)


def kernel(x, pre_norm_w, post_norm_w, gdn_w_in, gdn_conv_w, gdn_A_log, gdn_dt_bias, gdn_norm_w, gdn_w_out,
           sc_w_in, sc_conv_w, sc_w_out):
    x = gdn_layer(x, pre_norm_w[0], post_norm_w[0], gdn_w_in[0], gdn_conv_w[0], gdn_A_log[0],
                  gdn_dt_bias[0], gdn_norm_w[0], gdn_w_out[0])
    x = sc_layer(x, pre_norm_w[1], post_norm_w[1], sc_w_in[0], sc_conv_w[0], sc_w_out[0])
    return x
```

```python
import numpy as np
import jax
import jax.numpy as jnp
from jax import lax
from jax.experimental import pallas as pl
from jax.experimental.pallas import tpu as pltpu

D = 1024
EPS = 1e-6
HK, HV, DK, DV = 8, 16, 128, 128
QK = HK * DK
V = HV * DV
CONVD = 2 * QK + V
C = 64
NP = HV // 2
TT = 512
NCT = TT // C
P1_CHUNKS = 2
PG = 512
W = 2048
TTS = 512
CG = 512
LANES = 128
SUBLANES = 8
VMEM_LIMIT = 60 * 1024 * 1024
f32 = jnp.float32
bf16 = jnp.bfloat16


def _dot(a, b, **kw):
    return jnp.dot(a, b, preferred_element_type=f32, **kw)


def _rms(x, w):
    return x * lax.rsqrt(jnp.mean(x * x, axis=-1, keepdims=True) + EPS) * w


def _split3(a):
    hi = a.astype(bf16)
    r1 = a - hi.astype(f32)
    mid = r1.astype(bf16)
    lo = (r1 - mid.astype(f32)).astype(bf16)
    return hi, mid, lo


def _split3_rows(a):
    return jnp.concatenate(_split3(a), axis=0)


def _fold3_rows(r):
    n = r.shape[0] // 3
    return r[:n] + r[n:2 * n] + r[2 * n:]


def _dot3_rows(a, m):
    return _fold3_rows(_dot(_split3_rows(a), m))


def _interleave(*gens):
    gens = list(gens)
    while gens:
        for g in list(gens):
            try:
                next(g)
            except StopIteration:
                gens.remove(g)


def gdn_kernel(x_ref, prew_ref, postw_ref, wqkvz_ref, wb_ref, wa_ref, wbT_ref, waT_ref, cw_ref,
               alog_row_ref, dt_row_ref, alog_col_ref, dt_col_ref,
               dup_ref, u2_ref, ones2_ref, ltri_ref, cmask_ref, lvl_ref,
               normw_ref, wout_ref,
               o_ref,
               xp_s, z_s, q_s, k_s, kT2_s, v_s, S_s, lh1_s, lh2_s,
               bcol_s, gcol_s, brow_s, gcrow_s, glrow_s, og_s):
    t = pl.program_id(1)

    @pl.when(t == 0)
    def _():
        S_s[...] = jnp.zeros(S_s.shape, f32)
        xp_s[pl.ds(0, SUBLANES), :] = jnp.zeros((SUBLANES, CONVD), f32)

    x = x_ref[0]
    hb = _rms(x, prew_ref[...]).astype(bf16)

    nt = (((1,), (1,)), ((), ()))
    b_col = _dot(hb, wb_ref[...])
    a_col = _dot(hb, wa_ref[...])
    bT = lax.dot_general(wbT_ref[...], hb, nt, preferred_element_type=f32)
    aT = lax.dot_general(waT_ref[...], hb, nt, preferred_element_type=f32)
    g_col = -jnp.exp(alog_row_ref[...]) * jax.nn.softplus(a_col + dt_row_ref[...])
    gT = -jnp.exp(alog_col_ref[...]) * jax.nn.softplus(aT + dt_col_ref[...])
    betaT = jax.nn.sigmoid(bT)
    bcol_s[...] = jax.nn.sigmoid(b_col)
    gc_parts = _split3(g_col)
    gcol_s[...] = sum(_dot(ltri_ref[...], part) for part in gc_parts)
    brow_s[...] = _dot3_rows(betaT, dup_ref[...])
    gT3 = _split3_rows(gT)
    gcrow_s[...] = _fold3_rows(_dot(gT3, u2_ref[...]))
    glrow_s[...] = _fold3_rows(_dot(gT3, ones2_ref[...]))

    def l2n(a):
        return a * lax.rsqrt(jnp.sum(a * a, axis=-1, keepdims=True) + EPS)

    def project(groups):
        for j in groups:
            gs = slice(j * PG, (j + 1) * PG)
            if j * PG >= CONVD:
                zz = _dot(hb, wqkvz_ref[:, gs])
                z_s[:, j * PG - CONVD:(j + 1) * PG - CONVD] = zz * jax.nn.sigmoid(zz)
                yield
                continue
            xp_s[pl.ds(SUBLANES, TT), gs] = _dot(hb, wqkvz_ref[:, gs])
            yield
            for i in range(PG // LANES):
                col = j * PG + i * LANES
                cs = slice(col, col + LANES)
                acc = cw_ref[3:4, cs] * xp_s[pl.ds(SUBLANES, TT), cs]
                acc += cw_ref[2:3, cs] * xp_s[pl.ds(SUBLANES - 1, TT), cs]
                acc += cw_ref[1:2, cs] * xp_s[pl.ds(SUBLANES - 2, TT), cs]
                acc += cw_ref[0:1, cs] * xp_s[pl.ds(SUBLANES - 3, TT), cs]
                a = acc * jax.nn.sigmoid(acc)
                if col < QK:
                    q_s[:, cs] = (l2n(a) * (DK ** -0.5)).astype(bf16)
                elif col < 2 * QK:
                    g = (col - QK) // DK
                    kn = l2n(a)
                    k_s[:, col - QK:col - QK + LANES] = kn.astype(bf16)
                    for c in range(NCT):
                        kc = kn[c * C:(c + 1) * C]
                        kT2_s[g, :, c * LANES:(c + 1) * LANES] = jnp.concatenate([kc, kc], axis=0).T.astype(bf16)
                else:
                    v_s[:, col - 2 * QK:col - 2 * QK + LANES] = a.astype(bf16)
                yield

    n_qk = 2 * QK // PG
    for _ in project(range(n_qk)):
        pass

    lowm = cmask_ref[0]
    strictm = cmask_ref[1]
    diagm = cmask_ref[2]
    left = cmask_ref[3]
    right = cmask_ref[4]
    leftb, rightb = left.astype(bf16), right.astype(bf16)

    def bd1(a):
        return jnp.concatenate([a * leftb, a * rightb], axis=0)

    def pair_rows(ref, p, c):
        cs = slice(c * LANES, (c + 1) * LANES)
        return ref[2 * p:2 * p + 1, cs], ref[2 * p + 1:2 * p + 2, cs]

    def phase1(chains):
        Abs, Xs, tails = [], [], []
        for c, p in chains:
            rs = slice(c * C, (c + 1) * C)
            cl = slice(c * LANES, (c + 1) * LANES)
            hs = slice(p * DK, (p + 1) * DK)
            qk = jnp.concatenate([q_s[rs, hs], k_s[rs, hs]], axis=0)
            G = _dot(qk, kT2_s[p, :, cl])
            QKp, KKp = G[:C], G[C:]
            l1 = left[0:1]
            grow = jnp.where(l1 > 0, gcrow_s[2 * p:2 * p + 1, cl], gcrow_s[2 * p + 1:2 * p + 2, cl])
            brow = jnp.where(l1 > 0, brow_s[2 * p:2 * p + 1, cl], brow_s[2 * p + 1:2 * p + 2, cl])
            gcol = jnp.where(left > 0, gcol_s[rs, 2 * p:2 * p + 1], gcol_s[rs, 2 * p + 1:2 * p + 2])
            bcol = jnp.where(left > 0, bcol_s[rs, 2 * p:2 * p + 1], bcol_s[rs, 2 * p + 1:2 * p + 2])
            Dm = jnp.exp((gcol - grow) * lowm) * lowm
            Ap = bcol * KKp * Dm * strictm
            Abs.append(Ap.astype(bf16))
            Xs.append(diagm - Ap * lvl_ref[0].astype(f32))
            eg = jnp.exp(grow)
            lh2_s[c, p] = jnp.concatenate([QKp * Dm, diagm * eg], axis=1).astype(bf16)
            tails.append((brow, eg))
            if len(Xs) % 4 == 0:
                yield
        for lv in range(1, 6):
            Xbs = [X.astype(bf16) for X in Xs]
            Tms = [_dot(Ab * lvl_ref[lv], bd1(Xb)).astype(bf16) for Ab, Xb in zip(Abs, Xbs)]
            yield
            Xs = [X - _dot(Xb, bd1(Tm)) for X, Xb, Tm in zip(Xs, Xbs, Tms)]
            yield
        for n, ((c, p), X, (brow, eg)) in enumerate(zip(chains, Xs, tails)):
            Yb = X * brow
            lh1_s[c, p] = jnp.concatenate([Yb, -(Yb * eg)], axis=1).astype(bf16)
            if n % 8 == 7:
                yield

    def phase1_all():
        for c0 in range(0, NCT, P1_CHUNKS):
            yield from phase1([(c, p) for c in range(c0, c0 + P1_CHUNKS) for p in range(NP)])

    _interleave(phase1_all(), project(range(n_qk, (CONVD + V) // PG)))

    xp_s[pl.ds(0, SUBLANES), :] = xp_s[pl.ds(TT, SUBLANES), :]

    zero = jnp.zeros((C, LANES), bf16)

    def bd(a):
        return jnp.concatenate([jnp.concatenate([a[:, :LANES], zero], axis=1),
                                jnp.concatenate([zero, a[:, LANES:]], axis=1)], axis=0)

    def phase2(c):
        rs = slice(c * C, (c + 1) * C)
        cl = slice(c * LANES, (c + 1) * LANES)
        pairs = range(NP)
        Ss = [S_s[p] for p in pairs]
        qkSs = [_dot(jnp.concatenate([q_s[rs, p * DK:(p + 1) * DK], k_s[rs, p * DK:(p + 1) * DK]], axis=0),
                     Ss[p].astype(bf16)) for p in pairs]
        vnews = [_dot(lh1_s[c, p], jnp.concatenate([bd(v_s[rs, p * 2 * DV:(p + 1) * 2 * DV]),
                                                    bd(qkSs[p][C:].astype(bf16))], axis=0)).astype(bf16)
                 for p in pairs]
        vbds = [bd(vn) for vn in vnews]
        os_ = [_dot(lh2_s[c, p], jnp.concatenate([vbds[p], bd(qkSs[p][:C].astype(bf16))], axis=0)) for p in pairs]
        l1 = left[0:1]
        for p in pairs:
            gr0, gr1 = pair_rows(gcrow_s, p, c)
            gl0, gl1 = pair_rows(glrow_s, p, c)
            r = jnp.exp(jnp.where(l1 > 0, gl0 - gr0, gl1 - gr1))
            kTr = (kT2_s[p, :, cl].astype(f32) * r).astype(bf16)
            dec = jnp.concatenate([jnp.exp(gl0), jnp.exp(gl1)], axis=1)
            S_s[p] = Ss[p] * dec + _dot(kTr, vbds[p])
        for p in pairs:
            for hh in range(2):
                oh = os_[p][:, hh * DV:(hh + 1) * DV]
                zc = slice(p * 2 * DV + hh * DV, p * 2 * DV + (hh + 1) * DV)
                og_s[rs, zc] = (_rms(oh, normw_ref[...]) * z_s[rs, zc]).astype(bf16)

    for c in range(NCT):
        phase2(c)

    y = _dot(og_s[...], wout_ref[...])
    o_ref[0] = x + _rms(y, postw_ref[...])


def _gdn_constants():
    tt = np.arange(TT)
    m2 = np.arange(2 * TT)
    src = (m2 // LANES) * C + (m2 % C)
    same = (tt[:, None] // C) == (src[None, :] // C)
    dup = (tt[:, None] == src[None, :]).astype(np.float32)
    u2 = (same & (tt[:, None] <= src[None, :])).astype(np.float32)
    ones2 = same.astype(np.float32)
    ltri = (((tt[:, None] // C) == (tt[None, :] // C)) & (tt[None, :] <= tt[:, None])).astype(np.float32)
    r = np.arange(C)[:, None]
    l = np.arange(LANES)[None, :]
    cm = np.stack([(r >= l % C), (r > l % C), (r == l % C), np.broadcast_to(l < C, (C, LANES)),
                   np.broadcast_to(l >= C, (C, LANES))]).astype(np.float32)
    j = l % C
    lv = np.stack([((r >> (s + 1)) == (j >> (s + 1))) & ((r >> s) != (j >> s)) & (r > j) for s in range(6)])
    return dup, u2, ones2, ltri, cm, lv.astype(np.float32)


def _const_spec(a):
    nd = a.ndim
    return pl.BlockSpec(a.shape, lambda b, t, _n=nd: (0,) * _n, pipeline_mode=pl.Buffered(1))


def gdn_layer(x, prew, postw, w_in, conv_w, a_log, dt_bias, norm_w, w_out):
    B, T, _ = x.shape
    assert T % TT == 0
    dup, u2, ones2, ltri, cm, lv = _gdn_constants()
    wqkvz = w_in[:, :CONVD + V].astype(bf16)
    wb = w_in[:, CONVD + V:CONVD + V + HV].astype(bf16)
    wa = w_in[:, CONVD + V + HV:].astype(bf16)
    args = (x, prew.reshape(1, D), postw.reshape(1, D), wqkvz, wb, wa, wb.T, wa.T, conv_w,
            a_log.reshape(1, HV), dt_bias.reshape(1, HV), a_log.reshape(HV, 1), dt_bias.reshape(HV, 1),
            jnp.asarray(dup, dtype=bf16), jnp.asarray(u2, dtype=bf16), jnp.asarray(ones2, dtype=bf16),
            jnp.asarray(ltri, dtype=bf16), jnp.asarray(cm),
            jnp.asarray(lv, dtype=bf16), norm_w.reshape(1, DV), w_out.astype(bf16))
    in_specs = [pl.BlockSpec((1, TT, D), lambda b, t: (b, t, 0))] + [_const_spec(a) for a in args[1:]]
    return pl.pallas_call(
        gdn_kernel,
        grid=(B, T // TT),
        in_specs=in_specs,
        out_specs=pl.BlockSpec((1, TT, D), lambda b, t: (b, t, 0)),
        out_shape=jax.ShapeDtypeStruct(x.shape, f32),
        scratch_shapes=[
            pltpu.VMEM((TT + SUBLANES, CONVD), f32),
            pltpu.VMEM((TT, V), f32),
            pltpu.VMEM((TT, QK), bf16),
            pltpu.VMEM((TT, QK), bf16),
            pltpu.VMEM((HK, DK, NCT * LANES), bf16),
            pltpu.VMEM((TT, V), bf16),
            pltpu.VMEM((NP, DK, 2 * DV), f32),
            pltpu.VMEM((NCT, NP, C, 2 * LANES), bf16),
            pltpu.VMEM((NCT, NP, C, 2 * LANES), bf16),
            pltpu.VMEM((TT, HV), f32),
            pltpu.VMEM((TT, HV), f32),
            pltpu.VMEM((HV, 2 * TT), f32),
            pltpu.VMEM((HV, 2 * TT), f32),
            pltpu.VMEM((HV, 2 * TT), f32),
            pltpu.VMEM((TT, V), bf16)],
        compiler_params=pltpu.CompilerParams(dimension_semantics=("arbitrary", "arbitrary"),
                                             vmem_limit_bytes=VMEM_LIMIT),
        name="gdn_layer",
    )(*args)


def sc_kernel(x_ref, prew_ref, postw_ref, win_ref, cw_ref, wout_ref, o_ref, cu_s, y_s):
    t = pl.program_id(1)

    @pl.when(t == 0)
    def _():
        cu_s[pl.ds(0, SUBLANES), :] = jnp.zeros((SUBLANES, W), f32)

    x = x_ref[0]
    hb = _rms(x, prew_ref[...]).astype(bf16)
    for j in range(W // CG):
        cs = slice(j * CG, (j + 1) * CG)
        u = _dot(hb, win_ref[:, j * CG:(j + 1) * CG])
        cg = _dot(hb, win_ref[:, 2 * W + j * CG:2 * W + (j + 1) * CG])
        cu_s[pl.ds(SUBLANES, TTS), cs] = cg * u
        bg = _dot(hb, win_ref[:, W + j * CG:W + (j + 1) * CG])
        z = _dot(hb, win_ref[:, 3 * W + j * CG:3 * W + (j + 1) * CG])
        conv = (cw_ref[2:3, cs] * cu_s[pl.ds(SUBLANES, TTS), cs]
                + cw_ref[1:2, cs] * cu_s[pl.ds(SUBLANES - 1, TTS), cs]
                + cw_ref[0:1, cs] * cu_s[pl.ds(SUBLANES - 2, TTS), cs])
        y_s[:, cs] = (bg * conv * (z * jax.nn.sigmoid(z))).astype(bf16)
    cu_s[pl.ds(0, SUBLANES), :] = cu_s[pl.ds(TTS, SUBLANES), :]
    yo = _dot(y_s[...], wout_ref[...])
    o_ref[0] = x + _rms(yo, postw_ref[...])


def sc_layer(x, prew, postw, w_in, conv_w, w_out):
    B, T, _ = x.shape
    assert T % TTS == 0
    args = (x, prew.reshape(1, D), postw.reshape(1, D), w_in.astype(bf16), conv_w, w_out.astype(bf16))
    in_specs = [pl.BlockSpec((1, TTS, D), lambda b, t: (b, t, 0))] + [_const_spec(a) for a in args[1:]]
    return pl.pallas_call(
        sc_kernel,
        grid=(B, T // TTS),
        in_specs=in_specs,
        out_specs=pl.BlockSpec((1, TTS, D), lambda b, t: (b, t, 0)),
        out_shape=jax.ShapeDtypeStruct(x.shape, f32),
        scratch_shapes=[pltpu.VMEM((TTS + SUBLANES, W), f32), pltpu.VMEM((TTS, W), bf16)],
        compiler_params=pltpu.CompilerParams(dimension_semantics=("arbitrary", "arbitrary"),
                                             vmem_limit_bytes=VMEM_LIMIT),
        name="sc_layer",
    )(*args)


def kernel(x, pre_norm_w, post_norm_w, gdn_w_in, gdn_conv_w, gdn_A_log, gdn_dt_bias, gdn_norm_w, gdn_w_out,
           sc_w_in, sc_conv_w, sc_w_out):
    x = gdn_layer(x, pre_norm_w[0], post_norm_w[0], gdn_w_in[0], gdn_conv_w[0], gdn_A_log[0],
                  gdn_dt_bias[0], gdn_norm_w[0], gdn_w_out[0])
    x = sc_layer(x, pre_norm_w[1], post_norm_w[1], sc_w_in[0], sc_conv_w[0], sc_w_out[0])
    return x
```

```python
import numpy as np
import jax
import jax.numpy as jnp
from jax import lax
from jax.experimental import pallas as pl
from jax.experimental.pallas import tpu as pltpu

D = 1024
EPS = 1e-6
HK, HV, DK, DV = 8, 16, 128, 128
QK = HK * DK
V = HV * DV
CONVD = 2 * QK + V
C = 64
NP = HV // 2
TT = 512
NCT = TT // C
P1_CHUNKS = 2
PG = 512
W = 2048
TTS = 512
CG = 512
LANES = 128
SUBLANES = 8
VMEM_LIMIT = 60 * 1024 * 1024
f32 = jnp.float32
bf16 = jnp.bfloat16


def _dot(a, b, **kw):
    return jnp.dot(a, b, preferred_element_type=f32, **kw)


def _rms(x, w):
    return x * lax.rsqrt(jnp.mean(x * x, axis=-1, keepdims=True) + EPS) * w


def _split3(a):
    hi = a.astype(bf16)
    r1 = a - hi.astype(f32)
    mid = r1.astype(bf16)
    lo = (r1 - mid.astype(f32)).astype(bf16)
    return hi, mid, lo


def _split3_rows(a):
    return jnp.concatenate(_split3(a), axis=0)


def _fold3_rows(r):
    n = r.shape[0] // 3
    return r[:n] + r[n:2 * n] + r[2 * n:]


def _dot3_rows(a, m):
    return _fold3_rows(_dot(_split3_rows(a), m))


def _interleave(*gens):
    gens = list(gens)
    while gens:
        for g in list(gens):
            try:
                next(g)
            except StopIteration:
                gens.remove(g)


def gdn_kernel(x_ref, prew_ref, postw_ref, wqkvz_ref, wb_ref, wa_ref, wbT_ref, waT_ref, cw_ref,
               alog_row_ref, dt_row_ref, alog_col_ref, dt_col_ref,
               dup_ref, u2_ref, ones2_ref, ltri_ref, cmask_ref, lvl_ref,
               normw_ref, wout_ref,
               o_ref,
               xp_s, z_s, q_s, k_s, kT2_s, v_s, S_s, lh1_s, lh2_s,
               bcol_s, gcol_s, brow_s, gcrow_s, glrow_s, og_s):
    t = pl.program_id(1)

    @pl.when(t == 0)
    def _():
        S_s[...] = jnp.zeros(S_s.shape, f32)
        xp_s[pl.ds(0, SUBLANES), :] = jnp.zeros((SUBLANES, CONVD), f32)

    x = x_ref[0]
    hb = _rms(x, prew_ref[...]).astype(bf16)

    nt = (((1,), (1,)), ((), ()))
    b_col = _dot(hb, wb_ref[...])
    a_col = _dot(hb, wa_ref[...])
    bT = lax.dot_general(wbT_ref[...], hb, nt, preferred_element_type=f32)
    aT = lax.dot_general(waT_ref[...], hb, nt, preferred_element_type=f32)
    g_col = -jnp.exp(alog_row_ref[...]) * jax.nn.softplus(a_col + dt_row_ref[...])
    gT = -jnp.exp(alog_col_ref[...]) * jax.nn.softplus(aT + dt_col_ref[...])
    betaT = jax.nn.sigmoid(bT)
    bcol_s[...] = jax.nn.sigmoid(b_col)
    gc_parts = _split3(g_col)
    gcol_s[...] = sum(_dot(ltri_ref[...], part) for part in gc_parts)
    brow_s[...] = _dot3_rows(betaT, dup_ref[...])
    gT3 = _split3_rows(gT)
    gcrow_s[...] = _fold3_rows(_dot(gT3, u2_ref[...]))
    glrow_s[...] = _fold3_rows(_dot(gT3, ones2_ref[...]))

    def l2n(a):
        return a * lax.rsqrt(jnp.sum(a * a, axis=-1, keepdims=True) + EPS)

    def project(groups):
        for j in groups:
            gs = slice(j * PG, (j + 1) * PG)
            if j * PG >= CONVD:
                zz = _dot(hb, wqkvz_ref[:, gs])
                z_s[:, j * PG - CONVD:(j + 1) * PG - CONVD] = zz * jax.nn.sigmoid(zz)
                yield
                continue
            xp_s[pl.ds(SUBLANES, TT), gs] = _dot(hb, wqkvz_ref[:, gs])
            yield
            for i in range(PG // LANES):
                col = j * PG + i * LANES
                cs = slice(col, col + LANES)
                acc = cw_ref[3:4, cs] * xp_s[pl.ds(SUBLANES, TT), cs]
                acc += cw_ref[2:3, cs] * xp_s[pl.ds(SUBLANES - 1, TT), cs]
                acc += cw_ref[1:2, cs] * xp_s[pl.ds(SUBLANES - 2, TT), cs]
                acc += cw_ref[0:1, cs] * xp_s[pl.ds(SUBLANES - 3, TT), cs]
                a = acc * jax.nn.sigmoid(acc)
                if col < QK:
                    q_s[:, cs] = (l2n(a) * (DK ** -0.5)).astype(bf16)
                elif col < 2 * QK:
                    g = (col - QK) // DK
                    kn = l2n(a)
                    k_s[:, col - QK:col - QK + LANES] = kn.astype(bf16)
                    for c in range(NCT):
                        kc = kn[c * C:(c + 1) * C]
                        kT2_s[g, :, c * LANES:(c + 1) * LANES] = jnp.concatenate([kc, kc], axis=0).T.astype(bf16)
                else:
                    v_s[:, col - 2 * QK:col - 2 * QK + LANES] = a.astype(bf16)
                yield

    n_qk = 2 * QK // PG
    for _ in project(range(n_qk)):
        pass

    lowm = cmask_ref[0]
    strictm = cmask_ref[1]
    diagm = cmask_ref[2]
    left = cmask_ref[3]
    right = cmask_ref[4]
    leftb, rightb = left.astype(bf16), right.astype(bf16)

    def bd1(a):
        return jnp.concatenate([a * leftb, a * rightb], axis=0)

    def pair_rows(ref, p, c):
        cs = slice(c * LANES, (c + 1) * LANES)
        return ref[2 * p:2 * p + 1, cs], ref[2 * p + 1:2 * p + 2, cs]

    def phase1(chains):
        Abs, Xs, tails = [], [], []
        for c, p in chains:
            rs = slice(c * C, (c + 1) * C)
            cl = slice(c * LANES, (c + 1) * LANES)
            hs = slice(p * DK, (p + 1) * DK)
            qk = jnp.concatenate([q_s[rs, hs], k_s[rs, hs]], axis=0)
            G = _dot(qk, kT2_s[p, :, cl])
            QKp, KKp = G[:C], G[C:]
            l1 = left[0:1]
            grow = jnp.where(l1 > 0, gcrow_s[2 * p:2 * p + 1, cl], gcrow_s[2 * p + 1:2 * p + 2, cl])
            brow = jnp.where(l1 > 0, brow_s[2 * p:2 * p + 1, cl], brow_s[2 * p + 1:2 * p + 2, cl])
            gcol = jnp.where(left > 0, gcol_s[rs, 2 * p:2 * p + 1], gcol_s[rs, 2 * p + 1:2 * p + 2])
            bcol = jnp.where(left > 0, bcol_s[rs, 2 * p:2 * p + 1], bcol_s[rs, 2 * p + 1:2 * p + 2])
            Dm = jnp.exp((gcol - grow) * lowm) * lowm
            Ap = bcol * KKp * Dm * strictm
            Abs.append(Ap.astype(bf16))
            Xs.append(diagm - Ap * lvl_ref[0].astype(f32))
            eg = jnp.exp(grow)
            lh2_s[c, p] = jnp.concatenate([QKp * Dm, diagm * eg], axis=1).astype(bf16)
            tails.append((brow, eg))
            if len(Xs) % 4 == 0:
                yield
        for lv in range(1, 6):
            Xbs = [X.astype(bf16) for X in Xs]
            Tms = [_dot(Ab * lvl_ref[lv], bd1(Xb)).astype(bf16) for Ab, Xb in zip(Abs, Xbs)]
            yield
            Xs = [X - _dot(Xb, bd1(Tm)) for X, Xb, Tm in zip(Xs, Xbs, Tms)]
            yield
        for n, ((c, p), X, (brow, eg)) in enumerate(zip(chains, Xs, tails)):
            Yb = X * brow
            lh1_s[c, p] = jnp.concatenate([Yb, -(Yb * eg)], axis=1).astype(bf16)
            if n % 8 == 7:
                yield

    def phase1_all():
        for c0 in range(0, NCT, P1_CHUNKS):
            yield from phase1([(c, p) for c in range(c0, c0 + P1_CHUNKS) for p in range(NP)])

    _interleave(phase1_all(), project(range(n_qk, (CONVD + V) // PG)))

    xp_s[pl.ds(0, SUBLANES), :] = xp_s[pl.ds(TT, SUBLANES), :]

    zero = jnp.zeros((C, LANES), bf16)

    def bd(a):
        return jnp.concatenate([jnp.concatenate([a[:, :LANES], zero], axis=1),
                                jnp.concatenate([zero, a[:, LANES:]], axis=1)], axis=0)

    def phase2(c):
        rs = slice(c * C, (c + 1) * C)
        cl = slice(c * LANES, (c + 1) * LANES)
        pairs = range(NP)
        Ss = [S_s[p] for p in pairs]
        qkSs = [_dot(jnp.concatenate([q_s[rs, p * DK:(p + 1) * DK], k_s[rs, p * DK:(p + 1) * DK]], axis=0),
                     Ss[p].astype(bf16)) for p in pairs]
        vnews = [_dot(lh1_s[c, p], jnp.concatenate([bd(v_s[rs, p * 2 * DV:(p + 1) * 2 * DV]),
                                                    bd(qkSs[p][C:].astype(bf16))], axis=0)).astype(bf16)
                 for p in pairs]
        vbds = [bd(vn) for vn in vnews]
        os_ = [_dot(lh2_s[c, p], jnp.concatenate([vbds[p], bd(qkSs[p][:C].astype(bf16))], axis=0)) for p in pairs]
        l1 = left[0:1]
        for p in pairs:
            gr0, gr1 = pair_rows(gcrow_s, p, c)
            gl0, gl1 = pair_rows(glrow_s, p, c)
            r = jnp.exp(jnp.where(l1 > 0, gl0 - gr0, gl1 - gr1))
            kTr = (kT2_s[p, :, cl].astype(f32) * r).astype(bf16)
            dec = jnp.concatenate([jnp.exp(gl0), jnp.exp(gl1)], axis=1)
            S_s[p] = Ss[p] * dec + _dot(kTr, vbds[p])
        for p in pairs:
            for hh in range(2):
                oh = os_[p][:, hh * DV:(hh + 1) * DV]
                zc = slice(p * 2 * DV + hh * DV, p * 2 * DV + (hh + 1) * DV)
                og_s[rs, zc] = (_rms(oh, normw_ref[...]) * z_s[rs, zc]).astype(bf16)

    for c in range(NCT):
        phase2(c)

    y = _dot(og_s[...], wout_ref[...])
    o_ref[0] = x + _rms(y, postw_ref[...])


def _gdn_constants():
    tt = np.arange(TT)
    m2 = np.arange(2 * TT)
    src = (m2 // LANES) * C + (m2 % C)
    same = (tt[:, None] // C) == (src[None, :] // C)
    dup = (tt[:, None] == src[None, :]).astype(np.float32)
    u2 = (same & (tt[:, None] <= src[None, :])).astype(np.float32)
    ones2 = same.astype(np.float32)
    ltri = (((tt[:, None] // C) == (tt[None, :] // C)) & (tt[None, :] <= tt[:, None])).astype(np.float32)
    r = np.arange(C)[:, None]
    l = np.arange(LANES)[None, :]
    cm = np.stack([(r >= l % C), (r > l % C), (r == l % C), np.broadcast_to(l < C, (C, LANES)),
                   np.broadcast_to(l >= C, (C, LANES))]).astype(np.float32)
    j = l % C
    lv = np.stack([((r >> (s + 1)) == (j >> (s + 1))) & ((r >> s) != (j >> s)) & (r > j) for s in range(6)])
    return dup, u2, ones2, ltri, cm, lv.astype(np.float32)


def _const_spec(a):
    nd = a.ndim
    return pl.BlockSpec(a.shape, lambda b, t, _n=nd: (0,) * _n, pipeline_mode=pl.Buffered(1))


def gdn_layer(x, prew, postw, w_in, conv_w, a_log, dt_bias, norm_w, w_out):
    B, T, _ = x.shape
    assert T % TT == 0
    dup, u2, ones2, ltri, cm, lv = _gdn_constants()
    wqkvz = w_in.astype(bf16)
    wb = wqkvz[:, CONVD + V:CONVD + V + HV]
    wa = wqkvz[:, CONVD + V + HV:]
    args = (x, prew.reshape(1, D), postw.reshape(1, D), wqkvz, wb, wa, wb.T, wa.T, conv_w,
            a_log.reshape(1, HV), dt_bias.reshape(1, HV), a_log.reshape(HV, 1), dt_bias.reshape(HV, 1),
            jnp.asarray(dup, dtype=bf16), jnp.asarray(u2, dtype=bf16), jnp.asarray(ones2, dtype=bf16),
            jnp.asarray(ltri, dtype=bf16), jnp.asarray(cm),
            jnp.asarray(lv, dtype=bf16), norm_w.reshape(1, DV), w_out.astype(bf16))
    in_specs = [pl.BlockSpec((1, TT, D), lambda b, t: (b, t, 0))] + [_const_spec(a) for a in args[1:]]
    return pl.pallas_call(
        gdn_kernel,
        grid=(B, T // TT),
        in_specs=in_specs,
        out_specs=pl.BlockSpec((1, TT, D), lambda b, t: (b, t, 0)),
        out_shape=jax.ShapeDtypeStruct(x.shape, f32),
        scratch_shapes=[
            pltpu.VMEM((TT + SUBLANES, CONVD), f32),
            pltpu.VMEM((TT, V), f32),
            pltpu.VMEM((TT, QK), bf16),
            pltpu.VMEM((TT, QK), bf16),
            pltpu.VMEM((HK, DK, NCT * LANES), bf16),
            pltpu.VMEM((TT, V), bf16),
            pltpu.VMEM((NP, DK, 2 * DV), f32),
            pltpu.VMEM((NCT, NP, C, 2 * LANES), bf16),
            pltpu.VMEM((NCT, NP, C, 2 * LANES), bf16),
            pltpu.VMEM((TT, HV), f32),
            pltpu.VMEM((TT, HV), f32),
            pltpu.VMEM((HV, 2 * TT), f32),
            pltpu.VMEM((HV, 2 * TT), f32),
            pltpu.VMEM((HV, 2 * TT), f32),
            pltpu.VMEM((TT, V), bf16)],
        compiler_params=pltpu.CompilerParams(dimension_semantics=("arbitrary", "arbitrary"),
                                             vmem_limit_bytes=VMEM_LIMIT),
        name="gdn_layer",
    )(*args)


def sc_kernel(x_ref, prew_ref, postw_ref, win_ref, cw_ref, wout_ref, o_ref, cu_s, y_s):
    t = pl.program_id(1)

    @pl.when(t == 0)
    def _():
        cu_s[pl.ds(0, SUBLANES), :] = jnp.zeros((SUBLANES, W), f32)

    x = x_ref[0]
    hb = _rms(x, prew_ref[...]).astype(bf16)
    for j in range(W // CG):
        cs = slice(j * CG, (j + 1) * CG)
        u = _dot(hb, win_ref[:, j * CG:(j + 1) * CG])
        cg = _dot(hb, win_ref[:, 2 * W + j * CG:2 * W + (j + 1) * CG])
        cu_s[pl.ds(SUBLANES, TTS), cs] = cg * u
        bg = _dot(hb, win_ref[:, W + j * CG:W + (j + 1) * CG])
        z = _dot(hb, win_ref[:, 3 * W + j * CG:3 * W + (j + 1) * CG])
        conv = (cw_ref[2:3, cs] * cu_s[pl.ds(SUBLANES, TTS), cs]
                + cw_ref[1:2, cs] * cu_s[pl.ds(SUBLANES - 1, TTS), cs]
                + cw_ref[0:1, cs] * cu_s[pl.ds(SUBLANES - 2, TTS), cs])
        y_s[:, cs] = (bg * conv * (z * jax.nn.sigmoid(z))).astype(bf16)
    cu_s[pl.ds(0, SUBLANES), :] = cu_s[pl.ds(TTS, SUBLANES), :]
    yo = _dot(y_s[...], wout_ref[...])
    o_ref[0] = x + _rms(yo, postw_ref[...])


def sc_layer(x, prew, postw, w_in, conv_w, w_out):
    B, T, _ = x.shape
    assert T % TTS == 0
    args = (x, prew.reshape(1, D), postw.reshape(1, D), w_in.astype(bf16), conv_w, w_out.astype(bf16))
    in_specs = [pl.BlockSpec((1, TTS, D), lambda b, t: (b, t, 0))] + [_const_spec(a) for a in args[1:]]
    return pl.pallas_call(
        sc_kernel,
        grid=(B, T // TTS),
        in_specs=in_specs,
        out_specs=pl.BlockSpec((1, TTS, D), lambda b, t: (b, t, 0)),
        out_shape=jax.ShapeDtypeStruct(x.shape, f32),
        scratch_shapes=[pltpu.VMEM((TTS + SUBLANES, W), f32), pltpu.VMEM((TTS, W), bf16)],
        compiler_params=pltpu.CompilerParams(dimension_semantics=("arbitrary", "arbitrary"),
                                             vmem_limit_bytes=VMEM_LIMIT),
        name="sc_layer",
    )(*args)


def kernel(x, pre_norm_w, post_norm_w, gdn_w_in, gdn_conv_w, gdn_A_log, gdn_dt_bias, gdn_norm_w, gdn_w_out,
           sc_w_in, sc_conv_w, sc_w_out):
    x = gdn_layer(x, pre_norm_w[0], post_norm_w[0], gdn_w_in[0], gdn_conv_w[0], gdn_A_log[0],
                  gdn_dt_bias[0], gdn_norm_w[0], gdn_w_out[0])
    x = sc_layer(x, pre_norm_w[1], post_norm_w[1], sc_w_in[0], sc_conv_w[0], sc_w_out[0])
    return x
```

```python
import numpy as np
import jax
import jax.numpy as jnp
from jax import lax
from jax.experimental import pallas as pl
from jax.experimental.pallas import tpu as pltpu

D = 1024
EPS = 1e-6
HK, HV, DK, DV = 8, 16, 128, 128
QK = HK * DK
V = HV * DV
CONVD = 2 * QK + V
C = 64
NP = HV // 2
TT = 512
NCT = TT // C
P1_CHUNKS = 2
PG = 512
W = 2048
TTS = 512
CG = 512
LANES = 128
SUBLANES = 8
VMEM_LIMIT = 60 * 1024 * 1024
f32 = jnp.float32
bf16 = jnp.bfloat16


def _dot(a, b, **kw):
    return jnp.dot(a, b, preferred_element_type=f32, **kw)


def _rms(x, w):
    return x * lax.rsqrt(jnp.mean(x * x, axis=-1, keepdims=True) + EPS) * w


def _split3(a):
    hi = a.astype(bf16)
    r1 = a - hi.astype(f32)
    mid = r1.astype(bf16)
    lo = (r1 - mid.astype(f32)).astype(bf16)
    return hi, mid, lo


def _split3_rows(a):
    return jnp.concatenate(_split3(a), axis=0)


def _fold3_rows(r):
    n = r.shape[0] // 3
    return r[:n] + r[n:2 * n] + r[2 * n:]


def _dot3_rows(a, m):
    return _fold3_rows(_dot(_split3_rows(a), m))


def _interleave(*gens):
    gens = list(gens)
    while gens:
        for g in list(gens):
            try:
                next(g)
            except StopIteration:
                gens.remove(g)


def gdn_kernel(x_ref, prew_ref, postw_ref, wqkvz_ref, wb_ref, wa_ref, wbT_ref, waT_ref, cw_ref,
               alog_row_ref, dt_row_ref, alog_col_ref, dt_col_ref,
               dup_ref, u2_ref, ones2_ref, ltri_ref, cmask_ref, lvl_ref,
               normw_ref, wout_ref,
               o_ref,
               xp_s, z_s, q_s, k_s, kT2_s, v_s, S_s, lh1_s, lh2_s,
               bcol_s, gcol_s, brow_s, gcrow_s, glrow_s, og_s):
    t = pl.program_id(1)

    @pl.when(t == 0)
    def _():
        S_s[...] = jnp.zeros(S_s.shape, f32)
        xp_s[pl.ds(0, SUBLANES), :] = jnp.zeros((SUBLANES, CONVD), f32)

    x = x_ref[0]
    hb = _rms(x, prew_ref[...]).astype(bf16)

    nt = (((1,), (1,)), ((), ()))
    b_col = _dot(hb, wb_ref[...])
    a_col = _dot(hb, wa_ref[...])
    bT = lax.dot_general(wbT_ref[...], hb, nt, preferred_element_type=f32)
    aT = lax.dot_general(waT_ref[...], hb, nt, preferred_element_type=f32)
    g_col = -jnp.exp(alog_row_ref[...]) * jax.nn.softplus(a_col + dt_row_ref[...])
    gT = -jnp.exp(alog_col_ref[...]) * jax.nn.softplus(aT + dt_col_ref[...])
    betaT = jax.nn.sigmoid(bT)
    bcol_s[...] = jax.nn.sigmoid(b_col)
    gc_parts = _split3(g_col)
    gcol_s[...] = sum(_dot(ltri_ref[...], part) for part in gc_parts)
    brow_s[...] = _dot3_rows(betaT, dup_ref[...])
    gT3 = _split3_rows(gT)
    gcrow_s[...] = _fold3_rows(_dot(gT3, u2_ref[...]))
    glrow_s[...] = _fold3_rows(_dot(gT3, ones2_ref[...]))

    def l2n(a):
        return a * lax.rsqrt(jnp.sum(a * a, axis=-1, keepdims=True) + EPS)

    def project(groups):
        for j in groups:
            gs = slice(j * PG, (j + 1) * PG)
            if j * PG >= CONVD:
                zz = _dot(hb, wqkvz_ref[:, gs])
                z_s[:, j * PG - CONVD:(j + 1) * PG - CONVD] = zz * jax.nn.sigmoid(zz)
                yield
                continue
            xp_s[pl.ds(SUBLANES, TT), gs] = _dot(hb, wqkvz_ref[:, gs])
            yield
            for i in range(PG // LANES):
                col = j * PG + i * LANES
                cs = slice(col, col + LANES)
                acc = cw_ref[3:4, cs] * xp_s[pl.ds(SUBLANES, TT), cs]
                acc += cw_ref[2:3, cs] * xp_s[pl.ds(SUBLANES - 1, TT), cs]
                acc += cw_ref[1:2, cs] * xp_s[pl.ds(SUBLANES - 2, TT), cs]
                acc += cw_ref[0:1, cs] * xp_s[pl.ds(SUBLANES - 3, TT), cs]
                a = acc * jax.nn.sigmoid(acc)
                if col < QK:
                    q_s[:, cs] = (l2n(a) * (DK ** -0.5)).astype(bf16)
                elif col < 2 * QK:
                    g = (col - QK) // DK
                    kn = l2n(a)
                    k_s[:, col - QK:col - QK + LANES] = kn.astype(bf16)
                    for c in range(NCT):
                        kc = kn[c * C:(c + 1) * C]
                        kT2_s[g, :, c * LANES:(c + 1) * LANES] = jnp.concatenate([kc, kc], axis=0).T.astype(bf16)
                else:
                    v_s[:, col - 2 * QK:col - 2 * QK + LANES] = a.astype(bf16)
                yield

    n_qk = 2 * QK // PG
    for _ in project(range(n_qk)):
        pass

    lowm = cmask_ref[0]
    strictm = cmask_ref[1]
    diagm = cmask_ref[2]
    left = cmask_ref[3]
    right = cmask_ref[4]
    leftb, rightb = left.astype(bf16), right.astype(bf16)

    def bd1(a):
        return jnp.concatenate([a * leftb, a * rightb], axis=0)

    def pair_rows(ref, p, c):
        cs = slice(c * LANES, (c + 1) * LANES)
        return ref[2 * p:2 * p + 1, cs], ref[2 * p + 1:2 * p + 2, cs]

    def phase1(chains):
        Abs, Xs, tails = [], [], []
        for c, p in chains:
            rs = slice(c * C, (c + 1) * C)
            cl = slice(c * LANES, (c + 1) * LANES)
            hs = slice(p * DK, (p + 1) * DK)
            qk = jnp.concatenate([q_s[rs, hs], k_s[rs, hs]], axis=0)
            G = _dot(qk, kT2_s[p, :, cl])
            QKp, KKp = G[:C], G[C:]
            l1 = left[0:1]
            grow = jnp.where(l1 > 0, gcrow_s[2 * p:2 * p + 1, cl], gcrow_s[2 * p + 1:2 * p + 2, cl])
            brow = jnp.where(l1 > 0, brow_s[2 * p:2 * p + 1, cl], brow_s[2 * p + 1:2 * p + 2, cl])
            gcol = jnp.where(left > 0, gcol_s[rs, 2 * p:2 * p + 1], gcol_s[rs, 2 * p + 1:2 * p + 2])
            bcol = jnp.where(left > 0, bcol_s[rs, 2 * p:2 * p + 1], bcol_s[rs, 2 * p + 1:2 * p + 2])
            Dm = jnp.exp((gcol - grow) * lowm) * lowm
            Ap = bcol * KKp * Dm * strictm
            Abs.append(Ap.astype(bf16))
            Xs.append(diagm - Ap * lvl_ref[0].astype(f32))
            eg = jnp.exp(grow)
            lh2_s[c, p] = jnp.concatenate([QKp * Dm, diagm * eg], axis=1).astype(bf16)
            tails.append((brow, eg))
            if len(Xs) % 4 == 0:
                yield
        for lv in range(1, 6):
            Xbs = [X.astype(bf16) for X in Xs]
            Tms = [_dot(Ab * lvl_ref[lv], bd1(Xb)).astype(bf16) for Ab, Xb in zip(Abs, Xbs)]
            yield
            Xs = [X - _dot(Xb, bd1(Tm)) for X, Xb, Tm in zip(Xs, Xbs, Tms)]
            yield
        for n, ((c, p), X, (brow, eg)) in enumerate(zip(chains, Xs, tails)):
            Yb = X * brow
            lh1_s[c, p] = jnp.concatenate([Yb, -(Yb * eg)], axis=1).astype(bf16)
            if n % 8 == 7:
                yield

    def phase1_all():
        for c0 in range(0, NCT, P1_CHUNKS):
            yield from phase1([(c, p) for c in range(c0, c0 + P1_CHUNKS) for p in range(NP)])

    _interleave(phase1_all(), project(range(n_qk, (CONVD + V) // PG)))

    xp_s[pl.ds(0, SUBLANES), :] = xp_s[pl.ds(TT, SUBLANES), :]

    zero = jnp.zeros((C, LANES), bf16)

    def bd(a):
        return jnp.concatenate([jnp.concatenate([a[:, :LANES], zero], axis=1),
                                jnp.concatenate([zero, a[:, LANES:]], axis=1)], axis=0)

    def phase2(c):
        rs = slice(c * C, (c + 1) * C)
        cl = slice(c * LANES, (c + 1) * LANES)
        pairs = range(NP)
        Ss = [S_s[p] for p in pairs]
        qkSs = [_dot(jnp.concatenate([q_s[rs, p * DK:(p + 1) * DK], k_s[rs, p * DK:(p + 1) * DK]], axis=0),
                     Ss[p].astype(bf16)) for p in pairs]
        vnews = [_dot(lh1_s[c, p], jnp.concatenate([bd(v_s[rs, p * 2 * DV:(p + 1) * 2 * DV]),
                                                    bd(qkSs[p][C:].astype(bf16))], axis=0)).astype(bf16)
                 for p in pairs]
        vbds = [bd(vn) for vn in vnews]
        os_ = [_dot(lh2_s[c, p], jnp.concatenate([vbds[p], bd(qkSs[p][:C].astype(bf16))], axis=0)) for p in pairs]
        l1 = left[0:1]
        for p in pairs:
            gr0, gr1 = pair_rows(gcrow_s, p, c)
            gl0, gl1 = pair_rows(glrow_s, p, c)
            r = jnp.exp(jnp.where(l1 > 0, gl0 - gr0, gl1 - gr1))
            kTr = (kT2_s[p, :, cl].astype(f32) * r).astype(bf16)
            dec = jnp.concatenate([jnp.exp(gl0), jnp.exp(gl1)], axis=1)
            S_s[p] = Ss[p] * dec + _dot(kTr, vbds[p])
        for p in pairs:
            for hh in range(2):
                oh = os_[p][:, hh * DV:(hh + 1) * DV]
                zc = slice(p * 2 * DV + hh * DV, p * 2 * DV + (hh + 1) * DV)
                og_s[rs, zc] = (_rms(oh, normw_ref[...]) * z_s[rs, zc]).astype(bf16)

    for c in range(NCT):
        phase2(c)

    y = _dot(og_s[...], wout_ref[...])
    o_ref[0] = x + _rms(y, postw_ref[...])


def _gdn_constants():
    tt = np.arange(TT)
    m2 = np.arange(2 * TT)
    src = (m2 // LANES) * C + (m2 % C)
    same = (tt[:, None] // C) == (src[None, :] // C)
    dup = (tt[:, None] == src[None, :]).astype(np.float32)
    u2 = (same & (tt[:, None] <= src[None, :])).astype(np.float32)
    ones2 = same.astype(np.float32)
    ltri = (((tt[:, None] // C) == (tt[None, :] // C)) & (tt[None, :] <= tt[:, None])).astype(np.float32)
    r = np.arange(C)[:, None]
    l = np.arange(LANES)[None, :]
    cm = np.stack([(r >= l % C), (r > l % C), (r == l % C), np.broadcast_to(l < C, (C, LANES)),
                   np.broadcast_to(l >= C, (C, LANES))]).astype(np.float32)
    j = l % C
    lv = np.stack([((r >> (s + 1)) == (j >> (s + 1))) & ((r >> s) != (j >> s)) & (r > j) for s in range(6)])
    return dup, u2, ones2, ltri, cm, lv.astype(np.float32)


def _const_spec(a):
    nd = a.ndim
    return pl.BlockSpec(a.shape, lambda b, t, _n=nd: (0,) * _n, pipeline_mode=pl.Buffered(1))


def gdn_layer(x, prew, postw, w_in, conv_w, a_log, dt_bias, norm_w, w_out):
    B, T, _ = x.shape
    assert T % TT == 0
    dup, u2, ones2, ltri, cm, lv = _gdn_constants()
    wqkvz = w_in.astype(bf16)
    w_gate = lax.optimization_barrier(w_in[:, CONVD + V:]).astype(bf16)
    wb, wa = w_gate[:, :HV], w_gate[:, HV:]
    args = (x, prew.reshape(1, D), postw.reshape(1, D), wqkvz, wb, wa, wb.T, wa.T, conv_w,
            a_log.reshape(1, HV), dt_bias.reshape(1, HV), a_log.reshape(HV, 1), dt_bias.reshape(HV, 1),
            jnp.asarray(dup, dtype=bf16), jnp.asarray(u2, dtype=bf16), jnp.asarray(ones2, dtype=bf16),
            jnp.asarray(ltri, dtype=bf16), jnp.asarray(cm),
            jnp.asarray(lv, dtype=bf16), norm_w.reshape(1, DV), w_out.astype(bf16))
    in_specs = [pl.BlockSpec((1, TT, D), lambda b, t: (b, t, 0))] + [_const_spec(a) for a in args[1:]]
    return pl.pallas_call(
        gdn_kernel,
        grid=(B, T // TT),
        in_specs=in_specs,
        out_specs=pl.BlockSpec((1, TT, D), lambda b, t: (b, t, 0)),
        out_shape=jax.ShapeDtypeStruct(x.shape, f32),
        scratch_shapes=[
            pltpu.VMEM((TT + SUBLANES, CONVD), f32),
            pltpu.VMEM((TT, V), f32),
            pltpu.VMEM((TT, QK), bf16),
            pltpu.VMEM((TT, QK), bf16),
            pltpu.VMEM((HK, DK, NCT * LANES), bf16),
            pltpu.VMEM((TT, V), bf16),
            pltpu.VMEM((NP, DK, 2 * DV), f32),
            pltpu.VMEM((NCT, NP, C, 2 * LANES), bf16),
            pltpu.VMEM((NCT, NP, C, 2 * LANES), bf16),
            pltpu.VMEM((TT, HV), f32),
            pltpu.VMEM((TT, HV), f32),
            pltpu.VMEM((HV, 2 * TT), f32),
            pltpu.VMEM((HV, 2 * TT), f32),
            pltpu.VMEM((HV, 2 * TT), f32),
            pltpu.VMEM((TT, V), bf16)],
        compiler_params=pltpu.CompilerParams(dimension_semantics=("arbitrary", "arbitrary"),
                                             vmem_limit_bytes=VMEM_LIMIT),
        name="gdn_layer",
    )(*args)


def sc_kernel(x_ref, prew_ref, postw_ref, win_ref, cw_ref, wout_ref, o_ref, cu_s, y_s):
    t = pl.program_id(1)

    @pl.when(t == 0)
    def _():
        cu_s[pl.ds(0, SUBLANES), :] = jnp.zeros((SUBLANES, W), f32)

    x = x_ref[0]
    hb = _rms(x, prew_ref[...]).astype(bf16)
    for j in range(W // CG):
        cs = slice(j * CG, (j + 1) * CG)
        u = _dot(hb, win_ref[:, j * CG:(j + 1) * CG])
        cg = _dot(hb, win_ref[:, 2 * W + j * CG:2 * W + (j + 1) * CG])
        cu_s[pl.ds(SUBLANES, TTS), cs] = cg * u
        bg = _dot(hb, win_ref[:, W + j * CG:W + (j + 1) * CG])
        z = _dot(hb, win_ref[:, 3 * W + j * CG:3 * W + (j + 1) * CG])
        conv = (cw_ref[2:3, cs] * cu_s[pl.ds(SUBLANES, TTS), cs]
                + cw_ref[1:2, cs] * cu_s[pl.ds(SUBLANES - 1, TTS), cs]
                + cw_ref[0:1, cs] * cu_s[pl.ds(SUBLANES - 2, TTS), cs])
        y_s[:, cs] = (bg * conv * (z * jax.nn.sigmoid(z))).astype(bf16)
    cu_s[pl.ds(0, SUBLANES), :] = cu_s[pl.ds(TTS, SUBLANES), :]
    yo = _dot(y_s[...], wout_ref[...])
    o_ref[0] = x + _rms(yo, postw_ref[...])


def sc_layer(x, prew, postw, w_in, conv_w, w_out):
    B, T, _ = x.shape
    assert T % TTS == 0
    args = (x, prew.reshape(1, D), postw.reshape(1, D), w_in.astype(bf16), conv_w, w_out.astype(bf16))
    in_specs = [pl.BlockSpec((1, TTS, D), lambda b, t: (b, t, 0))] + [_const_spec(a) for a in args[1:]]
    return pl.pallas_call(
        sc_kernel,
        grid=(B, T // TTS),
        in_specs=in_specs,
        out_specs=pl.BlockSpec((1, TTS, D), lambda b, t: (b, t, 0)),
        out_shape=jax.ShapeDtypeStruct(x.shape, f32),
        scratch_shapes=[pltpu.VMEM((TTS + SUBLANES, W), f32), pltpu.VMEM((TTS, W), bf16)],
        compiler_params=pltpu.CompilerParams(dimension_semantics=("arbitrary", "arbitrary"),
                                             vmem_limit_bytes=VMEM_LIMIT),
        name="sc_layer",
    )(*args)


def kernel(x, pre_norm_w, post_norm_w, gdn_w_in, gdn_conv_w, gdn_A_log, gdn_dt_bias, gdn_norm_w, gdn_w_out,
           sc_w_in, sc_conv_w, sc_w_out):
    x = gdn_layer(x, pre_norm_w[0], post_norm_w[0], gdn_w_in[0], gdn_conv_w[0], gdn_A_log[0],
                  gdn_dt_bias[0], gdn_norm_w[0], gdn_w_out[0])
    x = sc_layer(x, pre_norm_w[1], post_norm_w[1], sc_w_in[0], sc_conv_w[0], sc_w_out[0])
    return x
```

```python
import numpy as np
import jax
import jax.numpy as jnp
from jax import lax
from jax.experimental import pallas as pl
from jax.experimental.pallas import tpu as pltpu

D = 1024
EPS = 1e-6
HK, HV, DK, DV = 8, 16, 128, 128
QK = HK * DK
V = HV * DV
CONVD = 2 * QK + V
C = 64
NP = HV // 2
TT = 512
NCT = TT // C
P1_CHUNKS = 2
PG = 512
W = 2048
TTS = 512
CG = 256
CAST_BLOCK = 512
LANES = 128
SUBLANES = 8
VMEM_LIMIT = 60 * 1024 * 1024
f32 = jnp.float32
bf16 = jnp.bfloat16


def _dot(a, b, **kw):
    return jnp.dot(a, b, preferred_element_type=f32, **kw)


def _rms(x, w):
    return x * lax.rsqrt(jnp.mean(x * x, axis=-1, keepdims=True) + EPS) * w


def _split3(a):
    hi = a.astype(bf16)
    r1 = a - hi.astype(f32)
    mid = r1.astype(bf16)
    lo = (r1 - mid.astype(f32)).astype(bf16)
    return hi, mid, lo


def _split3_rows(a):
    return jnp.concatenate(_split3(a), axis=0)


def _fold3_rows(r):
    n = r.shape[0] // 3
    return r[:n] + r[n:2 * n] + r[2 * n:]


def _dot3_rows(a, m):
    return _fold3_rows(_dot(_split3_rows(a), m))


def _interleave(*gens):
    gens = list(gens)
    while gens:
        for g in list(gens):
            try:
                next(g)
            except StopIteration:
                gens.remove(g)


def gdn_kernel(x_ref, prew_ref, postw_ref, wqkvz_ref, wb_ref, wa_ref, wbT_ref, waT_ref, cw_ref,
               alog_row_ref, dt_row_ref, alog_col_ref, dt_col_ref,
               dup_ref, u2_ref, ones2_ref, ltri_ref, cmask_ref, lvl_ref,
               normw_ref, wout_ref,
               o_ref,
               xp_s, z_s, q_s, k_s, kT2_s, v_s, S_s, lh1_s, lh2_s,
               bcol_s, gcol_s, brow_s, gcrow_s, glrow_s, og_s):
    t = pl.program_id(1)

    @pl.when(t == 0)
    def _():
        S_s[...] = jnp.zeros(S_s.shape, f32)
        xp_s[pl.ds(0, SUBLANES), :] = jnp.zeros((SUBLANES, CONVD), f32)

    x = x_ref[0]
    hb = _rms(x, prew_ref[...]).astype(bf16)

    nt = (((1,), (1,)), ((), ()))
    b_col = _dot(hb, wb_ref[...])
    a_col = _dot(hb, wa_ref[...])
    bT = lax.dot_general(wbT_ref[...], hb, nt, preferred_element_type=f32)
    aT = lax.dot_general(waT_ref[...], hb, nt, preferred_element_type=f32)
    g_col = -jnp.exp(alog_row_ref[...]) * jax.nn.softplus(a_col + dt_row_ref[...])
    gT = -jnp.exp(alog_col_ref[...]) * jax.nn.softplus(aT + dt_col_ref[...])
    betaT = jax.nn.sigmoid(bT)
    bcol_s[...] = jax.nn.sigmoid(b_col)
    gc_parts = _split3(g_col)
    gcol_s[...] = sum(_dot(ltri_ref[...], part) for part in gc_parts)
    brow_s[...] = _dot3_rows(betaT, dup_ref[...])
    gT3 = _split3_rows(gT)
    gcrow_s[...] = _fold3_rows(_dot(gT3, u2_ref[...]))
    glrow_s[...] = _fold3_rows(_dot(gT3, ones2_ref[...]))

    def l2n(a):
        return a * lax.rsqrt(jnp.sum(a * a, axis=-1, keepdims=True) + EPS)

    def project(groups):
        for j in groups:
            gs = slice(j * PG, (j + 1) * PG)
            if j * PG >= CONVD:
                zz = _dot(hb, wqkvz_ref[:, gs])
                z_s[:, j * PG - CONVD:(j + 1) * PG - CONVD] = zz * jax.nn.sigmoid(zz)
                yield
                continue
            xp_s[pl.ds(SUBLANES, TT), gs] = _dot(hb, wqkvz_ref[:, gs])
            yield
            for i in range(PG // LANES):
                col = j * PG + i * LANES
                cs = slice(col, col + LANES)
                acc = cw_ref[3:4, cs] * xp_s[pl.ds(SUBLANES, TT), cs]
                acc += cw_ref[2:3, cs] * xp_s[pl.ds(SUBLANES - 1, TT), cs]
                acc += cw_ref[1:2, cs] * xp_s[pl.ds(SUBLANES - 2, TT), cs]
                acc += cw_ref[0:1, cs] * xp_s[pl.ds(SUBLANES - 3, TT), cs]
                a = acc * jax.nn.sigmoid(acc)
                if col < QK:
                    q_s[:, cs] = (l2n(a) * (DK ** -0.5)).astype(bf16)
                elif col < 2 * QK:
                    g = (col - QK) // DK
                    kn = l2n(a)
                    k_s[:, col - QK:col - QK + LANES] = kn.astype(bf16)
                    for c in range(NCT):
                        kc = kn[c * C:(c + 1) * C]
                        kT2_s[g, :, c * LANES:(c + 1) * LANES] = jnp.concatenate([kc, kc], axis=0).T.astype(bf16)
                else:
                    v_s[:, col - 2 * QK:col - 2 * QK + LANES] = a.astype(bf16)
                yield

    n_qk = 2 * QK // PG
    for _ in project(range(n_qk)):
        pass

    lowm = cmask_ref[0]
    strictm = cmask_ref[1]
    diagm = cmask_ref[2]
    left = cmask_ref[3]
    right = cmask_ref[4]
    leftb, rightb = left.astype(bf16), right.astype(bf16)

    def bd1(a):
        return jnp.concatenate([a * leftb, a * rightb], axis=0)

    def pair_rows(ref, p, c):
        cs = slice(c * LANES, (c + 1) * LANES)
        return ref[2 * p:2 * p + 1, cs], ref[2 * p + 1:2 * p + 2, cs]

    def phase1(chains):
        Abs, Xs, tails = [], [], []
        for c, p in chains:
            rs = slice(c * C, (c + 1) * C)
            cl = slice(c * LANES, (c + 1) * LANES)
            hs = slice(p * DK, (p + 1) * DK)
            qk = jnp.concatenate([q_s[rs, hs], k_s[rs, hs]], axis=0)
            G = _dot(qk, kT2_s[p, :, cl])
            QKp, KKp = G[:C], G[C:]
            l1 = left[0:1]
            grow = jnp.where(l1 > 0, gcrow_s[2 * p:2 * p + 1, cl], gcrow_s[2 * p + 1:2 * p + 2, cl])
            brow = jnp.where(l1 > 0, brow_s[2 * p:2 * p + 1, cl], brow_s[2 * p + 1:2 * p + 2, cl])
            gcol = jnp.where(left > 0, gcol_s[rs, 2 * p:2 * p + 1], gcol_s[rs, 2 * p + 1:2 * p + 2])
            bcol = jnp.where(left > 0, bcol_s[rs, 2 * p:2 * p + 1], bcol_s[rs, 2 * p + 1:2 * p + 2])
            Dm = jnp.exp((gcol - grow) * lowm) * lowm
            Ap = bcol * KKp * Dm * strictm
            Abs.append(Ap.astype(bf16))
            Xs.append(diagm - Ap * lvl_ref[0].astype(f32))
            eg = jnp.exp(grow)
            lh2_s[c, p] = jnp.concatenate([QKp * Dm, diagm * eg], axis=1).astype(bf16)
            tails.append((brow, eg))
            if len(Xs) % 4 == 0:
                yield
        for lv in range(1, 6):
            Xbs = [X.astype(bf16) for X in Xs]
            Tms = [_dot(Ab * lvl_ref[lv], bd1(Xb)).astype(bf16) for Ab, Xb in zip(Abs, Xbs)]
            yield
            Xs = [X - _dot(Xb, bd1(Tm)) for X, Xb, Tm in zip(Xs, Xbs, Tms)]
            yield
        for n, ((c, p), X, (brow, eg)) in enumerate(zip(chains, Xs, tails)):
            Yb = X * brow
            lh1_s[c, p] = jnp.concatenate([Yb, -(Yb * eg)], axis=1).astype(bf16)
            if n % 8 == 7:
                yield

    def phase1_all():
        for c0 in range(0, NCT, P1_CHUNKS):
            yield from phase1([(c, p) for c in range(c0, c0 + P1_CHUNKS) for p in range(NP)])

    _interleave(phase1_all(), project(range(n_qk, (CONVD + V) // PG)))

    xp_s[pl.ds(0, SUBLANES), :] = xp_s[pl.ds(TT, SUBLANES), :]

    zero = jnp.zeros((C, LANES), bf16)

    def bd(a):
        return jnp.concatenate([jnp.concatenate([a[:, :LANES], zero], axis=1),
                                jnp.concatenate([zero, a[:, LANES:]], axis=1)], axis=0)

    def phase2(c):
        rs = slice(c * C, (c + 1) * C)
        cl = slice(c * LANES, (c + 1) * LANES)
        pairs = range(NP)
        Ss = [S_s[p] for p in pairs]
        qkSs = [_dot(jnp.concatenate([q_s[rs, p * DK:(p + 1) * DK], k_s[rs, p * DK:(p + 1) * DK]], axis=0),
                     Ss[p].astype(bf16)) for p in pairs]
        vnews = [_dot(lh1_s[c, p], jnp.concatenate([bd(v_s[rs, p * 2 * DV:(p + 1) * 2 * DV]),
                                                    bd(qkSs[p][C:].astype(bf16))], axis=0)).astype(bf16)
                 for p in pairs]
        vbds = [bd(vn) for vn in vnews]
        os_ = [_dot(lh2_s[c, p], jnp.concatenate([vbds[p], bd(qkSs[p][:C].astype(bf16))], axis=0)) for p in pairs]
        l1 = left[0:1]
        for p in pairs:
            gr0, gr1 = pair_rows(gcrow_s, p, c)
            gl0, gl1 = pair_rows(glrow_s, p, c)
            r = jnp.exp(jnp.where(l1 > 0, gl0 - gr0, gl1 - gr1))
            kTr = (kT2_s[p, :, cl].astype(f32) * r).astype(bf16)
            dec = jnp.concatenate([jnp.exp(gl0), jnp.exp(gl1)], axis=1)
            S_s[p] = Ss[p] * dec + _dot(kTr, vbds[p])
        for p in pairs:
            for hh in range(2):
                oh = os_[p][:, hh * DV:(hh + 1) * DV]
                zc = slice(p * 2 * DV + hh * DV, p * 2 * DV + (hh + 1) * DV)
                og_s[rs, zc] = (_rms(oh, normw_ref[...]) * z_s[rs, zc]).astype(bf16)

    for c in range(NCT):
        phase2(c)

    y = _dot(og_s[...], wout_ref[...])
    o_ref[0] = x + _rms(y, postw_ref[...])


def _gdn_constants():
    tt = np.arange(TT)
    m2 = np.arange(2 * TT)
    src = (m2 // LANES) * C + (m2 % C)
    same = (tt[:, None] // C) == (src[None, :] // C)
    dup = (tt[:, None] == src[None, :]).astype(np.float32)
    u2 = (same & (tt[:, None] <= src[None, :])).astype(np.float32)
    ones2 = same.astype(np.float32)
    ltri = (((tt[:, None] // C) == (tt[None, :] // C)) & (tt[None, :] <= tt[:, None])).astype(np.float32)
    r = np.arange(C)[:, None]
    l = np.arange(LANES)[None, :]
    cm = np.stack([(r >= l % C), (r > l % C), (r == l % C), np.broadcast_to(l < C, (C, LANES)),
                   np.broadcast_to(l >= C, (C, LANES))]).astype(np.float32)
    j = l % C
    lv = np.stack([((r >> (s + 1)) == (j >> (s + 1))) & ((r >> s) != (j >> s)) & (r > j) for s in range(6)])
    return dup, u2, ones2, ltri, cm, lv.astype(np.float32)


def _const_spec(a):
    nd = a.ndim
    return pl.BlockSpec(a.shape, lambda b, t, _n=nd: (0,) * _n, pipeline_mode=pl.Buffered(1))


def _cast_t_kernel(w_ref, o_ref):
    o_ref[...] = w_ref[...].T.astype(bf16)


def _cast_transposed(w_t, nrows):
    k = w_t.shape[1]
    return pl.pallas_call(
        _cast_t_kernel,
        grid=(nrows // CAST_BLOCK,),
        in_specs=[pl.BlockSpec((CAST_BLOCK, k), lambda i: (i, 0))],
        out_specs=pl.BlockSpec((k, CAST_BLOCK), lambda i: (0, i)),
        out_shape=jax.ShapeDtypeStruct((k, nrows), bf16),
        name="cast_weights",
    )(w_t)


def gdn_layer(x, prew, postw, w_in, conv_w, a_log, dt_bias, norm_w, w_out):
    B, T, _ = x.shape
    assert T % TT == 0
    dup, u2, ones2, ltri, cm, lv = _gdn_constants()
    w_t = w_in.T
    wqkvz = _cast_transposed(w_t, CONVD + V)
    w_gate_t = lax.optimization_barrier(w_t[CONVD + V:]).astype(bf16)
    wbT, waT = w_gate_t[:HV], w_gate_t[HV:]
    args = (x, prew.reshape(1, D), postw.reshape(1, D), wqkvz, wbT.T, waT.T, wbT, waT, conv_w,
            a_log.reshape(1, HV), dt_bias.reshape(1, HV), a_log.reshape(HV, 1), dt_bias.reshape(HV, 1),
            jnp.asarray(dup, dtype=bf16), jnp.asarray(u2, dtype=bf16), jnp.asarray(ones2, dtype=bf16),
            jnp.asarray(ltri, dtype=bf16), jnp.asarray(cm),
            jnp.asarray(lv, dtype=bf16), norm_w.reshape(1, DV), w_out.astype(bf16))
    in_specs = [pl.BlockSpec((1, TT, D), lambda b, t: (b, t, 0))] + [_const_spec(a) for a in args[1:]]
    return pl.pallas_call(
        gdn_kernel,
        grid=(B, T // TT),
        in_specs=in_specs,
        out_specs=pl.BlockSpec((1, TT, D), lambda b, t: (b, t, 0)),
        out_shape=jax.ShapeDtypeStruct(x.shape, f32),
        scratch_shapes=[
            pltpu.VMEM((TT + SUBLANES, CONVD), f32),
            pltpu.VMEM((TT, V), f32),
            pltpu.VMEM((TT, QK), bf16),
            pltpu.VMEM((TT, QK), bf16),
            pltpu.VMEM((HK, DK, NCT * LANES), bf16),
            pltpu.VMEM((TT, V), bf16),
            pltpu.VMEM((NP, DK, 2 * DV), f32),
            pltpu.VMEM((NCT, NP, C, 2 * LANES), bf16),
            pltpu.VMEM((NCT, NP, C, 2 * LANES), bf16),
            pltpu.VMEM((TT, HV), f32),
            pltpu.VMEM((TT, HV), f32),
            pltpu.VMEM((HV, 2 * TT), f32),
            pltpu.VMEM((HV, 2 * TT), f32),
            pltpu.VMEM((HV, 2 * TT), f32),
            pltpu.VMEM((TT, V), bf16)],
        compiler_params=pltpu.CompilerParams(dimension_semantics=("arbitrary", "arbitrary"),
                                             vmem_limit_bytes=VMEM_LIMIT),
        name="gdn_layer",
    )(*args)


def sc_kernel(x_ref, prew_ref, postw_ref, win_ref, cw_ref, wout_ref, o_ref, cu_s, y_s):
    t = pl.program_id(1)

    @pl.when(t == 0)
    def _():
        cu_s[pl.ds(0, SUBLANES), :] = jnp.zeros((SUBLANES, W), f32)

    x = x_ref[0]
    hb = _rms(x, prew_ref[...]).astype(bf16)
    for j in range(W // CG):
        cs = slice(j * CG, (j + 1) * CG)
        u = _dot(hb, win_ref[:, j * CG:(j + 1) * CG])
        cg = _dot(hb, win_ref[:, 2 * W + j * CG:2 * W + (j + 1) * CG])
        cu_s[pl.ds(SUBLANES, TTS), cs] = cg * u
        bg = _dot(hb, win_ref[:, W + j * CG:W + (j + 1) * CG])
        z = _dot(hb, win_ref[:, 3 * W + j * CG:3 * W + (j + 1) * CG])
        conv = (cw_ref[2:3, cs] * cu_s[pl.ds(SUBLANES, TTS), cs]
                + cw_ref[1:2, cs] * cu_s[pl.ds(SUBLANES - 1, TTS), cs]
                + cw_ref[0:1, cs] * cu_s[pl.ds(SUBLANES - 2, TTS), cs])
        y_s[:, cs] = (bg * conv * (z * jax.nn.sigmoid(z))).astype(bf16)
    cu_s[pl.ds(0, SUBLANES), :] = cu_s[pl.ds(TTS, SUBLANES), :]
    yo = _dot(y_s[...], wout_ref[...])
    o_ref[0] = x + _rms(yo, postw_ref[...])


def sc_layer(x, prew, postw, w_in, conv_w, w_out):
    B, T, _ = x.shape
    assert T % TTS == 0
    args = (x, prew.reshape(1, D), postw.reshape(1, D), w_in.astype(bf16), conv_w, w_out.astype(bf16))
    in_specs = [pl.BlockSpec((1, TTS, D), lambda b, t: (b, t, 0))] + [_const_spec(a) for a in args[1:]]
    return pl.pallas_call(
        sc_kernel,
        grid=(B, T // TTS),
        in_specs=in_specs,
        out_specs=pl.BlockSpec((1, TTS, D), lambda b, t: (b, t, 0)),
        out_shape=jax.ShapeDtypeStruct(x.shape, f32),
        scratch_shapes=[pltpu.VMEM((TTS + SUBLANES, W), f32), pltpu.VMEM((TTS, W), bf16)],
        compiler_params=pltpu.CompilerParams(dimension_semantics=("arbitrary", "arbitrary"),
                                             vmem_limit_bytes=VMEM_LIMIT),
        name="sc_layer",
    )(*args)


def kernel(x, pre_norm_w, post_norm_w, gdn_w_in, gdn_conv_w, gdn_A_log, gdn_dt_bias, gdn_norm_w, gdn_w_out,
           sc_w_in, sc_conv_w, sc_w_out):
    x = gdn_layer(x, pre_norm_w[0], post_norm_w[0], gdn_w_in[0], gdn_conv_w[0], gdn_A_log[0],
                  gdn_dt_bias[0], gdn_norm_w[0], gdn_w_out[0])
    x = sc_layer(x, pre_norm_w[1], post_norm_w[1], sc_w_in[0], sc_conv_w[0], sc_w_out[0])
    return x
```

```python
import numpy as np
import jax
import jax.numpy as jnp
from jax import lax
from jax.experimental import pallas as pl
from jax.experimental.pallas import tpu as pltpu

D = 1024
EPS = 1e-6
HK, HV, DK, DV = 8, 16, 128, 128
QK = HK * DK
V = HV * DV
CONVD = 2 * QK + V
C = 64
NP = HV // 2
TT = 512
NCT = TT // C
P1_CHUNKS = 2
PG = 256
W = 2048
TTS = 512
CG = 256
CAST_BLOCK = 1024
LANES = 128
SUBLANES = 8
VMEM_LIMIT = 60 * 1024 * 1024
f32 = jnp.float32
bf16 = jnp.bfloat16


def _dot(a, b, **kw):
    return jnp.dot(a, b, preferred_element_type=f32, **kw)


def _rms(x, w):
    return x * lax.rsqrt(jnp.mean(x * x, axis=-1, keepdims=True) + EPS) * w


def _split3(a):
    hi = a.astype(bf16)
    r1 = a - hi.astype(f32)
    mid = r1.astype(bf16)
    lo = (r1 - mid.astype(f32)).astype(bf16)
    return hi, mid, lo


def _split3_rows(a):
    return jnp.concatenate(_split3(a), axis=0)


def _fold3_rows(r):
    n = r.shape[0] // 3
    return r[:n] + r[n:2 * n] + r[2 * n:]


def _dot3_rows(a, m):
    return _fold3_rows(_dot(_split3_rows(a), m))


def _interleave(*gens):
    gens = list(gens)
    while gens:
        for g in list(gens):
            try:
                next(g)
            except StopIteration:
                gens.remove(g)


def gdn_kernel(x_ref, prew_ref, postw_ref, wqkvz_ref, wb_ref, wa_ref, wbT_ref, waT_ref, cw_ref,
               alog_row_ref, dt_row_ref, alog_col_ref, dt_col_ref,
               dup_ref, u2_ref, ones2_ref, ltri_ref, cmask_ref, lvl_ref,
               normw_ref, wout_ref,
               o_ref,
               xp_s, z_s, q_s, k_s, kT2_s, v_s, S_s, lh1_s, lh2_s,
               bcol_s, gcol_s, brow_s, gcrow_s, glrow_s, og_s):
    t = pl.program_id(1)

    @pl.when(t == 0)
    def _():
        S_s[...] = jnp.zeros(S_s.shape, f32)
        xp_s[pl.ds(0, SUBLANES), :] = jnp.zeros((SUBLANES, CONVD), f32)

    x = x_ref[0]
    hb = _rms(x, prew_ref[...]).astype(bf16)

    nt = (((1,), (1,)), ((), ()))
    b_col = _dot(hb, wb_ref[...])
    a_col = _dot(hb, wa_ref[...])
    bT = lax.dot_general(wbT_ref[...], hb, nt, preferred_element_type=f32)
    aT = lax.dot_general(waT_ref[...], hb, nt, preferred_element_type=f32)
    g_col = -jnp.exp(alog_row_ref[...]) * jax.nn.softplus(a_col + dt_row_ref[...])
    gT = -jnp.exp(alog_col_ref[...]) * jax.nn.softplus(aT + dt_col_ref[...])
    betaT = jax.nn.sigmoid(bT)
    bcol_s[...] = jax.nn.sigmoid(b_col)
    gc_parts = _split3(g_col)
    gcol_s[...] = sum(_dot(ltri_ref[...], part) for part in gc_parts)
    brow_s[...] = _dot3_rows(betaT, dup_ref[...])
    gT3 = _split3_rows(gT)
    gcrow_s[...] = _fold3_rows(_dot(gT3, u2_ref[...]))
    glrow_s[...] = _fold3_rows(_dot(gT3, ones2_ref[...]))

    def l2n(a):
        return a * lax.rsqrt(jnp.sum(a * a, axis=-1, keepdims=True) + EPS)

    def project(groups):
        for j in groups:
            gs = slice(j * PG, (j + 1) * PG)
            if j * PG >= CONVD:
                zz = _dot(hb, wqkvz_ref[:, gs])
                z_s[:, j * PG - CONVD:(j + 1) * PG - CONVD] = zz * jax.nn.sigmoid(zz)
                yield
                continue
            xp_s[pl.ds(SUBLANES, TT), gs] = _dot(hb, wqkvz_ref[:, gs])
            yield
            for i in range(PG // LANES):
                col = j * PG + i * LANES
                cs = slice(col, col + LANES)
                acc = cw_ref[3:4, cs] * xp_s[pl.ds(SUBLANES, TT), cs]
                acc += cw_ref[2:3, cs] * xp_s[pl.ds(SUBLANES - 1, TT), cs]
                acc += cw_ref[1:2, cs] * xp_s[pl.ds(SUBLANES - 2, TT), cs]
                acc += cw_ref[0:1, cs] * xp_s[pl.ds(SUBLANES - 3, TT), cs]
                a = acc * jax.nn.sigmoid(acc)
                if col < QK:
                    q_s[:, cs] = (l2n(a) * (DK ** -0.5)).astype(bf16)
                elif col < 2 * QK:
                    g = (col - QK) // DK
                    kn = l2n(a)
                    k_s[:, col - QK:col - QK + LANES] = kn.astype(bf16)
                    for c in range(NCT):
                        kc = kn[c * C:(c + 1) * C]
                        kT2_s[g, :, c * LANES:(c + 1) * LANES] = jnp.concatenate([kc, kc], axis=0).T.astype(bf16)
                else:
                    v_s[:, col - 2 * QK:col - 2 * QK + LANES] = a.astype(bf16)
                yield

    n_qk = 2 * QK // PG
    for _ in project(range(n_qk)):
        pass

    lowm = cmask_ref[0]
    strictm = cmask_ref[1]
    diagm = cmask_ref[2]
    left = cmask_ref[3]
    right = cmask_ref[4]
    leftb, rightb = left.astype(bf16), right.astype(bf16)

    def bd1(a):
        return jnp.concatenate([a * leftb, a * rightb], axis=0)

    def pair_rows(ref, p, c):
        cs = slice(c * LANES, (c + 1) * LANES)
        return ref[2 * p:2 * p + 1, cs], ref[2 * p + 1:2 * p + 2, cs]

    def phase1(chains):
        Abs, Xs, tails = [], [], []
        for c, p in chains:
            rs = slice(c * C, (c + 1) * C)
            cl = slice(c * LANES, (c + 1) * LANES)
            hs = slice(p * DK, (p + 1) * DK)
            qk = jnp.concatenate([q_s[rs, hs], k_s[rs, hs]], axis=0)
            G = _dot(qk, kT2_s[p, :, cl])
            QKp, KKp = G[:C], G[C:]
            l1 = left[0:1]
            grow = jnp.where(l1 > 0, gcrow_s[2 * p:2 * p + 1, cl], gcrow_s[2 * p + 1:2 * p + 2, cl])
            brow = jnp.where(l1 > 0, brow_s[2 * p:2 * p + 1, cl], brow_s[2 * p + 1:2 * p + 2, cl])
            gcol = jnp.where(left > 0, gcol_s[rs, 2 * p:2 * p + 1], gcol_s[rs, 2 * p + 1:2 * p + 2])
            bcol = jnp.where(left > 0, bcol_s[rs, 2 * p:2 * p + 1], bcol_s[rs, 2 * p + 1:2 * p + 2])
            Dm = jnp.exp((gcol - grow) * lowm) * lowm
            Ap = bcol * KKp * Dm * strictm
            Abs.append(Ap.astype(bf16))
            Xs.append(diagm - Ap * lvl_ref[0].astype(f32))
            eg = jnp.exp(grow)
            lh2_s[c, p] = jnp.concatenate([QKp * Dm, diagm * eg], axis=1).astype(bf16)
            tails.append((brow, eg))
            if len(Xs) % 4 == 0:
                yield
        for lv in range(1, 6):
            Xbs = [X.astype(bf16) for X in Xs]
            Tms = [_dot(Ab * lvl_ref[lv], bd1(Xb)).astype(bf16) for Ab, Xb in zip(Abs, Xbs)]
            yield
            Xs = [X - _dot(Xb, bd1(Tm)) for X, Xb, Tm in zip(Xs, Xbs, Tms)]
            yield
        for n, ((c, p), X, (brow, eg)) in enumerate(zip(chains, Xs, tails)):
            Yb = X * brow
            lh1_s[c, p] = jnp.concatenate([Yb, -(Yb * eg)], axis=1).astype(bf16)
            if n % 8 == 7:
                yield

    def phase1_all():
        for c0 in range(0, NCT, P1_CHUNKS):
            yield from phase1([(c, p) for c in range(c0, c0 + P1_CHUNKS) for p in range(NP)])

    _interleave(phase1_all(), project(range(n_qk, (CONVD + V) // PG)))

    xp_s[pl.ds(0, SUBLANES), :] = xp_s[pl.ds(TT, SUBLANES), :]

    zero = jnp.zeros((C, LANES), bf16)

    def bd(a):
        return jnp.concatenate([jnp.concatenate([a[:, :LANES], zero], axis=1),
                                jnp.concatenate([zero, a[:, LANES:]], axis=1)], axis=0)

    def phase2(c):
        rs = slice(c * C, (c + 1) * C)
        cl = slice(c * LANES, (c + 1) * LANES)
        pairs = range(NP)
        Ss = [S_s[p] for p in pairs]
        qkSs = [_dot(jnp.concatenate([q_s[rs, p * DK:(p + 1) * DK], k_s[rs, p * DK:(p + 1) * DK]], axis=0),
                     Ss[p].astype(bf16)) for p in pairs]
        vnews = [_dot(lh1_s[c, p], jnp.concatenate([bd(v_s[rs, p * 2 * DV:(p + 1) * 2 * DV]),
                                                    bd(qkSs[p][C:].astype(bf16))], axis=0)).astype(bf16)
                 for p in pairs]
        vbds = [bd(vn) for vn in vnews]
        os_ = [_dot(lh2_s[c, p], jnp.concatenate([vbds[p], bd(qkSs[p][:C].astype(bf16))], axis=0)) for p in pairs]
        l1 = left[0:1]
        for p in pairs:
            gr0, gr1 = pair_rows(gcrow_s, p, c)
            gl0, gl1 = pair_rows(glrow_s, p, c)
            r = jnp.exp(jnp.where(l1 > 0, gl0 - gr0, gl1 - gr1))
            kTr = (kT2_s[p, :, cl].astype(f32) * r).astype(bf16)
            dec = jnp.concatenate([jnp.exp(gl0), jnp.exp(gl1)], axis=1)
            S_s[p] = Ss[p] * dec + _dot(kTr, vbds[p])
        for p in pairs:
            for hh in range(2):
                oh = os_[p][:, hh * DV:(hh + 1) * DV]
                zc = slice(p * 2 * DV + hh * DV, p * 2 * DV + (hh + 1) * DV)
                og_s[rs, zc] = (_rms(oh, normw_ref[...]) * z_s[rs, zc]).astype(bf16)

    for c in range(NCT):
        phase2(c)

    y = _dot(og_s[...], wout_ref[...])
    o_ref[0] = x + _rms(y, postw_ref[...])


def _gdn_constants():
    tt = np.arange(TT)
    m2 = np.arange(2 * TT)
    src = (m2 // LANES) * C + (m2 % C)
    same = (tt[:, None] // C) == (src[None, :] // C)
    dup = (tt[:, None] == src[None, :]).astype(np.float32)
    u2 = (same & (tt[:, None] <= src[None, :])).astype(np.float32)
    ones2 = same.astype(np.float32)
    ltri = (((tt[:, None] // C) == (tt[None, :] // C)) & (tt[None, :] <= tt[:, None])).astype(np.float32)
    r = np.arange(C)[:, None]
    l = np.arange(LANES)[None, :]
    cm = np.stack([(r >= l % C), (r > l % C), (r == l % C), np.broadcast_to(l < C, (C, LANES)),
                   np.broadcast_to(l >= C, (C, LANES))]).astype(np.float32)
    j = l % C
    lv = np.stack([((r >> (s + 1)) == (j >> (s + 1))) & ((r >> s) != (j >> s)) & (r > j) for s in range(6)])
    return dup, u2, ones2, ltri, cm, lv.astype(np.float32)


def _const_spec(a):
    nd = a.ndim
    return pl.BlockSpec(a.shape, lambda b, t, _n=nd: (0,) * _n, pipeline_mode=pl.Buffered(1))


def _cast_t_kernel(w_ref, o_ref):
    o_ref[...] = w_ref[...].T.astype(bf16)


def _cast_transposed(w_t, nrows):
    k = w_t.shape[1]
    return pl.pallas_call(
        _cast_t_kernel,
        grid=(nrows // CAST_BLOCK,),
        in_specs=[pl.BlockSpec((CAST_BLOCK, k), lambda i: (i, 0))],
        out_specs=pl.BlockSpec((k, CAST_BLOCK), lambda i: (0, i)),
        out_shape=jax.ShapeDtypeStruct((k, nrows), bf16),
        compiler_params=pltpu.CompilerParams(dimension_semantics=("arbitrary",), vmem_limit_bytes=32 * 1024 * 1024),
        name="cast_weights",
    )(w_t)


def gdn_layer(x, prew, postw, w_in, conv_w, a_log, dt_bias, norm_w, w_out):
    B, T, _ = x.shape
    assert T % TT == 0
    dup, u2, ones2, ltri, cm, lv = _gdn_constants()
    w_t = w_in.T
    wqkvz = _cast_transposed(w_t, CONVD + V)
    w_gate_t = lax.optimization_barrier(w_t[CONVD + V:]).astype(bf16)
    wbT, waT = w_gate_t[:HV], w_gate_t[HV:]
    args = (x, prew.reshape(1, D), postw.reshape(1, D), wqkvz, wbT.T, waT.T, wbT, waT, conv_w,
            a_log.reshape(1, HV), dt_bias.reshape(1, HV), a_log.reshape(HV, 1), dt_bias.reshape(HV, 1),
            jnp.asarray(dup, dtype=bf16), jnp.asarray(u2, dtype=bf16), jnp.asarray(ones2, dtype=bf16),
            jnp.asarray(ltri, dtype=bf16), jnp.asarray(cm),
            jnp.asarray(lv, dtype=bf16), norm_w.reshape(1, DV), w_out.astype(bf16))
    in_specs = [pl.BlockSpec((1, TT, D), lambda b, t: (b, t, 0))] + [_const_spec(a) for a in args[1:]]
    return pl.pallas_call(
        gdn_kernel,
        grid=(B, T // TT),
        in_specs=in_specs,
        out_specs=pl.BlockSpec((1, TT, D), lambda b, t: (b, t, 0)),
        out_shape=jax.ShapeDtypeStruct(x.shape, f32),
        scratch_shapes=[
            pltpu.VMEM((TT + SUBLANES, CONVD), f32),
            pltpu.VMEM((TT, V), f32),
            pltpu.VMEM((TT, QK), bf16),
            pltpu.VMEM((TT, QK), bf16),
            pltpu.VMEM((HK, DK, NCT * LANES), bf16),
            pltpu.VMEM((TT, V), bf16),
            pltpu.VMEM((NP, DK, 2 * DV), f32),
            pltpu.VMEM((NCT, NP, C, 2 * LANES), bf16),
            pltpu.VMEM((NCT, NP, C, 2 * LANES), bf16),
            pltpu.VMEM((TT, HV), f32),
            pltpu.VMEM((TT, HV), f32),
            pltpu.VMEM((HV, 2 * TT), f32),
            pltpu.VMEM((HV, 2 * TT), f32),
            pltpu.VMEM((HV, 2 * TT), f32),
            pltpu.VMEM((TT, V), bf16)],
        compiler_params=pltpu.CompilerParams(dimension_semantics=("arbitrary", "arbitrary"),
                                             vmem_limit_bytes=VMEM_LIMIT),
        name="gdn_layer",
    )(*args)


def sc_kernel(x_ref, prew_ref, postw_ref, win_ref, cw_ref, wout_ref, o_ref, cu_s, y_s):
    t = pl.program_id(1)

    @pl.when(t == 0)
    def _():
        cu_s[pl.ds(0, SUBLANES), :] = jnp.zeros((SUBLANES, W), f32)

    x = x_ref[0]
    hb = _rms(x, prew_ref[...]).astype(bf16)
    for j in range(W // CG):
        cs = slice(j * CG, (j + 1) * CG)
        u = _dot(hb, win_ref[:, j * CG:(j + 1) * CG])
        cg = _dot(hb, win_ref[:, 2 * W + j * CG:2 * W + (j + 1) * CG])
        cu_s[pl.ds(SUBLANES, TTS), cs] = cg * u
        bg = _dot(hb, win_ref[:, W + j * CG:W + (j + 1) * CG])
        z = _dot(hb, win_ref[:, 3 * W + j * CG:3 * W + (j + 1) * CG])
        conv = (cw_ref[2:3, cs] * cu_s[pl.ds(SUBLANES, TTS), cs]
                + cw_ref[1:2, cs] * cu_s[pl.ds(SUBLANES - 1, TTS), cs]
                + cw_ref[0:1, cs] * cu_s[pl.ds(SUBLANES - 2, TTS), cs])
        y_s[:, cs] = (bg * conv * (z * jax.nn.sigmoid(z))).astype(bf16)
    cu_s[pl.ds(0, SUBLANES), :] = cu_s[pl.ds(TTS, SUBLANES), :]
    yo = _dot(y_s[...], wout_ref[...])
    o_ref[0] = x + _rms(yo, postw_ref[...])


def sc_layer(x, prew, postw, w_in, conv_w, w_out):
    B, T, _ = x.shape
    assert T % TTS == 0
    args = (x, prew.reshape(1, D), postw.reshape(1, D), w_in.astype(bf16), conv_w, w_out.astype(bf16))
    in_specs = [pl.BlockSpec((1, TTS, D), lambda b, t: (b, t, 0))] + [_const_spec(a) for a in args[1:]]
    return pl.pallas_call(
        sc_kernel,
        grid=(B, T // TTS),
        in_specs=in_specs,
        out_specs=pl.BlockSpec((1, TTS, D), lambda b, t: (b, t, 0)),
        out_shape=jax.ShapeDtypeStruct(x.shape, f32),
        scratch_shapes=[pltpu.VMEM((TTS + SUBLANES, W), f32), pltpu.VMEM((TTS, W), bf16)],
        compiler_params=pltpu.CompilerParams(dimension_semantics=("arbitrary", "arbitrary"),
                                             vmem_limit_bytes=VMEM_LIMIT),
        name="sc_layer",
    )(*args)


def kernel(x, pre_norm_w, post_norm_w, gdn_w_in, gdn_conv_w, gdn_A_log, gdn_dt_bias, gdn_norm_w, gdn_w_out,
           sc_w_in, sc_conv_w, sc_w_out):
    x = gdn_layer(x, pre_norm_w[0], post_norm_w[0], gdn_w_in[0], gdn_conv_w[0], gdn_A_log[0],
                  gdn_dt_bias[0], gdn_norm_w[0], gdn_w_out[0])
    x = sc_layer(x, pre_norm_w[1], post_norm_w[1], sc_w_in[0], sc_conv_w[0], sc_w_out[0])
    return x
```
